```python
import math
import jax, jax.numpy as jnp
from jax import lax
import numpy as np

D_MODEL = 1024
BATCH = 2
SEQ = 8192
DEPTH = 2
DEC_BATCH = 32
DEC_SEQ = 4
PAST_LEN = 16384
PAGE_SIZE = 128

N_MIXERS = 2
N_FOX = (DEPTH + 1) // 2
N_DIFF = DEPTH // 2
FOX_HEADS = 16
FOX_HEAD_DIM = D_MODEL // FOX_HEADS
FOX_WIDTH = FOX_HEADS * FOX_HEAD_DIM
FORGET_W_SCALE = 0.3
DIFF_HEADS = 8
DIFF_HEAD_DIM = D_MODEL // (2 * DIFF_HEADS)
DIFF_WIDTH = DIFF_HEADS * 2 * DIFF_HEAD_DIM
ROPE_THETA = 500000.0
ROPE_FRACTION = 4
Q_BLOCK = 128
N_GROUPS = 4
EXPERTS_PER_GROUP = 4
N_EXPERTS = N_GROUPS * EXPERTS_PER_GROUP
TOP_K = 2
EXPERT_FF = 512
PLE_DIM = 256
RMS_EPS = 1e-6

kernel_name = 'fox_diffattn_hier_moe_ple_step'


def rms_norm(x, gain):
    xf = x.astype(jnp.float32)
    y = xf * lax.rsqrt(jnp.mean(xf * xf, axis=-1, keepdims=True) + RMS_EPS)
    return (y * gain.astype(jnp.float32)).astype(x.dtype)


def partial_rope(x, pos):
    rot = x.shape[-1] // ROPE_FRACTION
    half = rot // 2
    inv_freq = jnp.power(jnp.float32(ROPE_THETA), -jnp.arange(half, dtype=jnp.float32) * (2.0 / rot))
    ang = pos.astype(jnp.float32)[:, None] * inv_freq[None, :]
    cos = jnp.cos(ang)[:, None, None, :]
    sin = jnp.sin(ang)[:, None, None, :]
    xr = x[..., :rot].astype(jnp.float32)
    x1, x2 = xr[..., :half], xr[..., half:]
    rotated = jnp.concatenate([x1 * cos - x2 * sin, x2 * cos + x1 * sin], axis=-1)
    return jnp.concatenate([rotated.astype(x.dtype), x[..., rot:]], axis=-1)


def sweep_query_blocks(block_fn, seq_len):
    starts = jnp.arange(seq_len // Q_BLOCK) * Q_BLOCK
    out = lax.map(block_fn, starts)
    out = jnp.moveaxis(out, 0, 1)
    return out.reshape((out.shape[0], seq_len) + out.shape[3:])


def fox_project(xn, w_in, b_f, q_gain, k_gain):
    b, t, _ = xn.shape
    proj = xn @ w_in
    q = proj[..., :FOX_WIDTH].reshape(b, t, FOX_HEADS, FOX_HEAD_DIM)
    k = proj[..., FOX_WIDTH:2 * FOX_WIDTH].reshape(b, t, FOX_HEADS, FOX_HEAD_DIM)
    v = proj[..., 2 * FOX_WIDTH:3 * FOX_WIDTH].reshape(b, t, FOX_HEADS, FOX_HEAD_DIM)
    logf = jax.nn.log_sigmoid(proj[..., 3 * FOX_WIDTH:].astype(jnp.float32) + b_f.astype(jnp.float32))
    return rms_norm(q, q_gain), rms_norm(k, k_gain), v, logf


def fox_prompt(q, k, v, logf):
    s_len = q.shape[1]
    scale = FOX_HEAD_DIM ** -0.5
    c = jnp.cumsum(logf.astype(jnp.float32), axis=1).transpose(0, 2, 1)
    kpos = jnp.arange(s_len)

    def block(start):
        qb = lax.dynamic_slice_in_dim(q, start, Q_BLOCK, axis=1)
        cb = lax.dynamic_slice_in_dim(c, start, Q_BLOCK, axis=2)
        s = jnp.einsum('bqhd,bkhd->bhqk', qb, k).astype(jnp.float32) * scale + cb[..., :, None] - c[..., None, :]
        mask = (start + jnp.arange(Q_BLOCK))[:, None] >= kpos[None, :]
        p = jax.nn.softmax(jnp.where(mask, s, -jnp.inf), axis=-1)
        return jnp.einsum('bhqk,bkhd->bqhd', p.astype(v.dtype), v)

    return sweep_query_blocks(block, s_len)


def fox_sample(q, k, v, logf, k_past, v_past, logf_past):
    t = q.shape[1]
    p_len = k_past.shape[1]
    scale = FOX_HEAD_DIM ** -0.5
    cn = jnp.cumsum(logf.astype(jnp.float32), axis=1).transpose(0, 2, 1)
    lp = logf_past.astype(jnp.float32)
    suffix = (lax.cumsum(lp, axis=1, reverse=True) - lp).transpose(0, 2, 1)
    s_past = jnp.einsum('bqhd,bkhd->bhqk', q, k_past).astype(jnp.float32) * scale + cn[..., :, None] + suffix[..., None, :]
    s_new = jnp.einsum('bqhd,bkhd->bhqk', q, k).astype(jnp.float32) * scale + cn[..., :, None] - cn[..., None, :]
    causal = jnp.tril(jnp.ones((t, t), dtype=bool))
    s_new = jnp.where(causal, s_new, -jnp.inf)
    p = jax.nn.softmax(jnp.concatenate([s_past, s_new], axis=-1), axis=-1)
    return (jnp.einsum('bhqk,bkhd->bqhd', p[..., :p_len].astype(v.dtype), v_past)
            + jnp.einsum('bhqk,bkhd->bqhd', p[..., p_len:].astype(v.dtype), v))


def diff_project(xn, w_in, q_gain, k_gain, pos):
    b, t, _ = xn.shape
    proj = xn @ w_in
    q = proj[..., :DIFF_WIDTH].reshape(b, t, DIFF_HEADS, 2, DIFF_HEAD_DIM)
    k = proj[..., DIFF_WIDTH:2 * DIFF_WIDTH].reshape(b, t, DIFF_HEADS, 2, DIFF_HEAD_DIM)
    v = proj[..., 2 * DIFF_WIDTH:].reshape(b, t, DIFF_HEADS, 2 * DIFF_HEAD_DIM)
    q = partial_rope(rms_norm(q, q_gain), pos)
    k = partial_rope(rms_norm(k, k_gain), pos)
    return q, k, v


def diff_lambda_value(lam_params, lam_init):
    lp = lam_params.astype(jnp.float32)
    return jnp.exp(jnp.sum(lp[0] * lp[1])) - jnp.exp(jnp.sum(lp[2] * lp[3])) + lam_init


def diff_prompt(q, k, v, lam):
    s_len = q.shape[1]
    scale = DIFF_HEAD_DIM ** -0.5
    kpos = jnp.arange(s_len)

    def block(start):
        qb = lax.dynamic_slice_in_dim(q, start, Q_BLOCK, axis=1)
        s = jnp.einsum('bqhcd,bkhcd->bhcqk', qb, k).astype(jnp.float32) * scale
        mask = (start + jnp.arange(Q_BLOCK))[:, None] >= kpos[None, :]
        p = jax.nn.softmax(jnp.where(mask, s, -jnp.inf), axis=-1)
        a = p[:, :, 0] - lam * p[:, :, 1]
        return jnp.einsum('bhqk,bkhe->bqhe', a.astype(v.dtype), v)

    return sweep_query_blocks(block, s_len)


def diff_sample(q, k, v, lam, k_past, v_past):
    t = q.shape[1]
    p_len = k_past.shape[1]
    scale = DIFF_HEAD_DIM ** -0.5
    s_past = jnp.einsum('bqhcd,bkhcd->bhcqk', q, k_past).astype(jnp.float32) * scale
    s_new = jnp.einsum('bqhcd,bkhcd->bhcqk', q, k).astype(jnp.float32) * scale
    causal = jnp.tril(jnp.ones((t, t), dtype=bool))
    s_new = jnp.where(causal, s_new, -jnp.inf)
    p = jax.nn.softmax(jnp.concatenate([s_past, s_new], axis=-1), axis=-1)
    a = p[:, :, 0] - lam * p[:, :, 1]
    return (jnp.einsum('bhqk,bkhe->bqhe', a[..., :p_len].astype(v.dtype), v_past)
            + jnp.einsum('bhqk,bkhe->bqhe', a[..., p_len:].astype(v.dtype), v))


def diff_output(o, subln, lam_init, w_out):
    o = rms_norm(o, subln) * (1.0 - lam_init)
    return o.reshape(o.shape[0], o.shape[1], -1) @ w_out


def hier_moe(x, w_group, b_group, w_expert, b_expert, w_gate, w_up, w_down):
    lead = x.shape[:-1]
    xt = x.reshape(-1, x.shape[-1])
    g_logits = (xt @ w_group).astype(jnp.float32) + b_group.astype(jnp.float32)
    g_prob = jax.nn.softmax(g_logits, axis=-1)
    g_sel = jnp.argmax(g_logits, axis=-1)
    g_onehot = jax.nn.one_hot(g_sel, N_GROUPS, dtype=jnp.float32)
    g_weight = jnp.sum(g_prob * g_onehot, axis=-1)
    e_logits = ((xt @ w_expert).astype(jnp.float32) + b_expert.astype(jnp.float32)).reshape(-1, N_GROUPS, EXPERTS_PER_GROUP)
    e_logits = jnp.einsum('nge,ng->ne', e_logits, g_onehot)
    e_prob = jax.nn.softmax(e_logits, axis=-1)
    top_w, top_i = lax.top_k(e_prob, TOP_K)
    top_w = top_w / jnp.sum(top_w, axis=-1, keepdims=True) * g_weight[:, None]
    expert_idx = g_sel[:, None] * EXPERTS_PER_GROUP + top_i
    gates = jnp.sum(jax.nn.one_hot(expert_idx, N_EXPERTS, dtype=jnp.float32) * top_w[..., None], axis=1).astype(x.dtype)
    out = jnp.zeros_like(xt)
    for e in range(N_EXPERTS):
        h = jax.nn.silu(xt @ w_gate[e]) * (xt @ w_up[e])
        out = out + (h @ w_down[e]) * gates[:, e:e + 1]
    return out.reshape(lead + (x.shape[-1],))


def per_layer_embedding(h, p, norm_gain, w_gate, w_proj):
    gate = jax.nn.sigmoid((rms_norm(h, norm_gain) @ w_gate).astype(jnp.float32)).astype(h.dtype)
    return gate * (p @ w_proj)


def setup_inputs(seed: int = 0) -> dict:
    key = jax.random.key(seed)
    keys = iter(jax.random.split(key, 48))

    def nrm(shape, scale=1.0):
        return jax.random.normal(next(keys), shape, jnp.float32) * scale

    def gain(shape):
        return 1.0 + nrm(shape, 0.02)

    n_pages = PAST_LEN // PAGE_SIZE
    in_use = DEC_BATCH * n_pages
    n_pool = in_use + max(1, in_use // 4)
    fbias_heads = jnp.linspace(1.0, 6.0, FOX_HEADS, dtype=jnp.float32)
    d = D_MODEL

    inputs = {}
    inputs['x_prompt'] = nrm((BATCH, SEQ, d))
    inputs['x_sample'] = nrm((DEC_BATCH, DEC_SEQ, d))
    inputs['p_prompt'] = nrm((DEPTH, BATCH, SEQ, PLE_DIM))
    inputs['p_sample'] = nrm((DEPTH, DEC_BATCH, DEC_SEQ, PLE_DIM))
    inputs['cache_fox_k'] = nrm((N_FOX, n_pool, PAGE_SIZE, FOX_HEADS, FOX_HEAD_DIM))
    inputs['cache_fox_v'] = nrm((N_FOX, n_pool, PAGE_SIZE, FOX_HEADS, FOX_HEAD_DIM))
    inputs['cache_fox_logf'] = jax.nn.log_sigmoid(fbias_heads + nrm((N_FOX, n_pool, PAGE_SIZE, FOX_HEADS), 0.5))
    inputs['cache_diff_k'] = nrm((N_DIFF, n_pool, PAGE_SIZE, DIFF_HEADS, 2, DIFF_HEAD_DIM))
    inputs['cache_diff_v'] = nrm((N_DIFF, n_pool, PAGE_SIZE, DIFF_HEADS, 2 * DIFF_HEAD_DIM))
    inputs['page_table'] = jax.random.permutation(next(keys), n_pool)[:in_use].reshape(DEC_BATCH, n_pages).astype(jnp.int32)
    inputs['attn_norm'] = gain((DEPTH, d))
    fox_w_in = nrm((N_FOX, d, 3 * FOX_WIDTH + FOX_HEADS), d ** -0.5)
    inputs['fox_w_in'] = fox_w_in.at[..., 3 * FOX_WIDTH:].multiply(FORGET_W_SCALE)
    inputs['fox_b_f'] = fbias_heads + nrm((N_FOX, FOX_HEADS), 0.5)
    inputs['fox_q_norm'] = gain((N_FOX, FOX_HEAD_DIM))
    inputs['fox_k_norm'] = gain((N_FOX, FOX_HEAD_DIM))
    inputs['fox_w_out'] = nrm((N_FOX, FOX_WIDTH, d), FOX_WIDTH ** -0.5)
    inputs['diff_w_in'] = nrm((N_DIFF, d, 3 * DIFF_WIDTH), d ** -0.5)
    inputs['diff_q_norm'] = gain((N_DIFF, DIFF_HEAD_DIM))
    inputs['diff_k_norm'] = gain((N_DIFF, DIFF_HEAD_DIM))
    inputs['diff_lambda'] = nrm((N_DIFF, 4, DIFF_HEAD_DIM), 0.1)
    inputs['diff_subln'] = gain((N_DIFF, 2 * DIFF_HEAD_DIM))
    inputs['diff_w_out'] = nrm((N_DIFF, DIFF_WIDTH, d), DIFF_WIDTH ** -0.5)
    inputs['ffn_norm'] = gain((DEPTH, d))
    inputs['moe_w_group'] = nrm((DEPTH, d, N_GROUPS), d ** -0.5)
    inputs['moe_b_group'] = nrm((DEPTH, N_GROUPS), 0.01)
    inputs['moe_w_expert'] = nrm((DEPTH, d, N_EXPERTS), d ** -0.5)
    inputs['moe_b_expert'] = nrm((DEPTH, N_EXPERTS), 0.01)
    inputs['moe_w_gate'] = nrm((DEPTH, N_EXPERTS, d, EXPERT_FF), d ** -0.5)
    inputs['moe_w_up'] = nrm((DEPTH, N_EXPERTS, d, EXPERT_FF), d ** -0.5)
    inputs['moe_w_down'] = nrm((DEPTH, N_EXPERTS, EXPERT_FF, d), EXPERT_FF ** -0.5)
    inputs['ple_norm'] = gain((DEPTH, d))
    inputs['ple_w_gate'] = nrm((DEPTH, d, d), d ** -0.5)
    inputs['ple_w_proj'] = nrm((DEPTH, PLE_DIM, d), PLE_DIM ** -0.5)
    return inputs


def reference(x_prompt, x_sample, p_prompt, p_sample, cache_fox_k, cache_fox_v, cache_fox_logf,
              cache_diff_k, cache_diff_v, page_table, attn_norm, fox_w_in, fox_b_f, fox_q_norm,
              fox_k_norm, fox_w_out, diff_w_in, diff_q_norm, diff_k_norm, diff_lambda, diff_subln,
              diff_w_out, ffn_norm, moe_w_group, moe_b_group, moe_w_expert, moe_b_expert,
              moe_w_gate, moe_w_up, moe_w_down, ple_norm, ple_w_gate, ple_w_proj):
    s_len = x_prompt.shape[1]
    t_len = x_sample.shape[1]
    dec_b = x_sample.shape[0]
    past_len = page_table.shape[1] * cache_fox_k.shape[2]
    pos_prompt = jnp.arange(s_len)
    pos_sample = past_len + jnp.arange(t_len)

    def gather(cache, j):
        rows = cache[j, page_table]
        return rows.reshape((dec_b, past_len) + rows.shape[3:])

    fk_p, fv_p, ff_p, dk_p, dv_p = [], [], [], [], []
    fk_s, fv_s, ff_s, dk_s, dv_s = [], [], [], [], []
    hp, hs = x_prompt, x_sample
    for i in range(DEPTH):
        j = i // N_MIXERS
        xp = rms_norm(hp, attn_norm[i])
        xs = rms_norm(hs, attn_norm[i])
        if i % N_MIXERS == 0:
            qp, kp, vp, lfp = fox_project(xp, fox_w_in[j], fox_b_f[j], fox_q_norm[j], fox_k_norm[j])
            qs, ks, vs, lfs = fox_project(xs, fox_w_in[j], fox_b_f[j], fox_q_norm[j], fox_k_norm[j])
            op = fox_prompt(qp, kp, vp, lfp)
            os_ = fox_sample(qs, ks, vs, lfs, gather(cache_fox_k, j), gather(cache_fox_v, j), gather(cache_fox_logf, j))
            hp = hp + op.reshape(op.shape[0], s_len, -1) @ fox_w_out[j]
            hs = hs + os_.reshape(dec_b, t_len, -1) @ fox_w_out[j]
            fk_p.append(kp); fv_p.append(vp); ff_p.append(lfp)
            fk_s.append(ks); fv_s.append(vs); ff_s.append(lfs)
        else:
            lam_init = 0.8 - 0.6 * math.exp(-0.3 * i)
            lam = diff_lambda_value(diff_lambda[j], lam_init)
            qp, kp, vp = diff_project(xp, diff_w_in[j], diff_q_norm[j], diff_k_norm[j], pos_prompt)
            qs, ks, vs = diff_project(xs, diff_w_in[j], diff_q_norm[j], diff_k_norm[j], pos_sample)
            op = diff_prompt(qp, kp, vp, lam)
            os_ = diff_sample(qs, ks, vs, lam, gather(cache_diff_k, j), gather(cache_diff_v, j))
            hp = hp + diff_output(op, diff_subln[j], lam_init, diff_w_out[j])
            hs = hs + diff_output(os_, diff_subln[j], lam_init, diff_w_out[j])
            dk_p.append(kp); dv_p.append(vp)
            dk_s.append(ks); dv_s.append(vs)
        hp = hp + hier_moe(rms_norm(hp, ffn_norm[i]), moe_w_group[i], moe_b_group[i], moe_w_expert[i],
                           moe_b_expert[i], moe_w_gate[i], moe_w_up[i], moe_w_down[i])
        hs = hs + hier_moe(rms_norm(hs, ffn_norm[i]), moe_w_group[i], moe_b_group[i], moe_w_expert[i],
                           moe_b_expert[i], moe_w_gate[i], moe_w_up[i], moe_w_down[i])
        hp = hp + per_layer_embedding(hp, p_prompt[i], ple_norm[i], ple_w_gate[i], ple_w_proj[i])
        hs = hs + per_layer_embedding(hs, p_sample[i], ple_norm[i], ple_w_gate[i], ple_w_proj[i])

    return (hp, hs,
            jnp.stack(fk_p), jnp.stack(fv_p), jnp.stack(ff_p), jnp.stack(dk_p), jnp.stack(dv_p),
            jnp.stack(fk_s), jnp.stack(fv_s), jnp.stack(ff_s), jnp.stack(dk_s), jnp.stack(dv_s))
```

```python
import functools
import math

import jax
import jax.numpy as jnp
from jax import lax
from jax.experimental import pallas as pl
from jax.experimental.pallas import tpu as pltpu

F32 = jnp.float32
BF16 = jnp.bfloat16

LANES = 128
MXU_COLS = 256
GROUP = 64
VMEM_LIMIT_BYTES = 56 * 1024 * 1024
RMS_EPS = 1e-6
ROPE_THETA = 500000.0
ROPE_FRACTION = 4
N_MIXERS = 2
TOP_K = 2

PROJ_ROWS = 512
ATTN_ROWS = 512
MOE_ROWS = 512
DECODE_PAGES_PER_STEP = 4


def _params(*sem):
    return pltpu.CompilerParams(dimension_semantics=sem, vmem_limit_bytes=VMEM_LIMIT_BYTES)


def _full(shape):
    zeros = (0,) * len(shape)
    return pl.BlockSpec(shape, lambda *_: zeros)


def _rows(tm, width):
    return pl.BlockSpec((tm, width), lambda i: (i, 0))


def _rms_rows(x, gain):
    return x * lax.rsqrt(jnp.mean(x * x, axis=-1, keepdims=True) + RMS_EPS) * gain


def _div_pow2(x, n):
    assert n & (n - 1) == 0
    return jnp.right_shift(x, n.bit_length() - 1)


def _split3(x):
    p1 = x.astype(BF16)
    r1 = x - p1.astype(F32)
    p2 = r1.astype(BF16)
    p3 = (r1 - p2.astype(F32)).astype(BF16)
    return p1, p2, p3


def _split2(x):
    hi = x.astype(BF16)
    return hi, (x - hi.astype(F32)).astype(BF16)


def _dot_pieces(xs, w_ref, cols=slice(None)):
    out = jnp.dot(xs[0], w_ref[0, :, cols], preferred_element_type=F32)
    if w_ref.shape[0] == 2:
        out = (out + jnp.dot(xs[1], w_ref[0, :, cols], preferred_element_type=F32)
               + jnp.dot(xs[0], w_ref[1, :, cols], preferred_element_type=F32))
    return out


def _group_scale(x, seg_ref, exp_ref, group):
    sq_hi, sq_lo = _split2(x * x)
    ss = (jnp.dot(sq_hi, seg_ref[...], preferred_element_type=F32)
          + jnp.dot(sq_lo, seg_ref[...], preferred_element_type=F32))
    hi, lo = _split2(lax.rsqrt(ss * (1.0 / group) + RMS_EPS))
    return jnp.dot(jnp.concatenate([hi, lo], axis=1), exp_ref[...], preferred_element_type=F32)


def _rope_tables(tile, tm, period, offset, invf):
    half = GROUP // ROPE_FRACTION // 2
    row = lax.broadcasted_iota(jnp.int32, (tm, LANES), 0) + tile * tm
    pos = (jnp.bitwise_and(row, period - 1) + offset).astype(F32)
    ang = pos * invf
    cs = jnp.cos(ang)
    sn = jnp.sin(ang)
    d = jnp.bitwise_and(lax.broadcasted_iota(jnp.int32, (tm, LANES), 1), GROUP - 1)
    return cs, jnp.where(d < half, -sn, 0.0), jnp.where(d >= half, sn, 0.0)


def _apply_rope(x, tables):
    cs, s_lo, s_hi = tables
    half = GROUP // ROPE_FRACTION // 2
    out = []
    for j in range(x.shape[1] // LANES):
        xs = x[:, j * LANES:(j + 1) * LANES]
        out.append(xs * cs + pltpu.roll(xs, LANES - half, 1) * s_lo + pltpu.roll(xs, half, 1) * s_hi)
    return jnp.concatenate(out, axis=1)


def _fox_proj_body(h_ref, an_ref, w_ref, wf_ref, bf_ref, qg_ref, kg_ref, seg_ref, exp_ref, tri_ref,
                   *rest, tm, d, nheads, period, prompt):
    if prompt:
        q_ref, kf_ref, vf_ref, lf_ref, c_ref, kb_ref, vb_ref, ct_ref, carry_ref = rest
    else:
        q_ref, kf_ref, vf_ref, lf_ref, c_ref, carry_ref = rest
    tile = pl.program_id(0)
    xn = _rms_rows(h_ref[...], an_ref[...])
    xb, x_lo = xs = _split2(xn)

    q = _dot_pieces(xs, w_ref, slice(0, d))
    q = q * _group_scale(q, seg_ref, exp_ref, GROUP) * qg_ref[...]
    q_ref[...] = (q * GROUP ** -0.5).astype(q_ref.dtype)
    k = _dot_pieces(xs, w_ref, slice(d, 2 * d))
    k = k * _group_scale(k, seg_ref, exp_ref, GROUP) * kg_ref[...]
    kf_ref[...] = k
    v = _dot_pieces(xs, w_ref, slice(2 * d, 3 * d))
    vf_ref[...] = v
    if prompt:
        kb_ref[...] = k.astype(BF16)
        vb_ref[...] = v.astype(BF16)

    z_hi = jnp.dot(xb, wf_ref[...], preferred_element_type=F32)
    z_lo = jnp.dot(x_lo, wf_ref[...], preferred_element_type=F32)
    z = z_hi + pltpu.roll(z_hi, LANES - nheads, 1) + z_lo + bf_ref[...]
    lf = jnp.minimum(z, 0.0) - jnp.log1p(jnp.exp(-jnp.abs(z)))
    lane = lax.broadcasted_iota(jnp.int32, (tm, LANES), 1)
    lf = jnp.where(lane < nheads, lf, 0.0)
    lf_ref[...] = lf[:, 0:nheads]

    p1, p2, p3 = _split3(lf)
    tri = tri_ref[...]
    c = (jnp.dot(tri, p1, preferred_element_type=F32) + jnp.dot(tri, p2, preferred_element_type=F32)
         + jnp.dot(tri, p3, preferred_element_type=F32))
    if period > tm:
        @pl.when(jnp.bitwise_and(tile * tm, period - 1) == 0)
        def _():
            carry_ref[...] = jnp.zeros_like(carry_ref)
        c = c + carry_ref[...]
        carry_ref[...] = c[tm - 1:tm, :]
    c_ref[...] = c[:, 0:nheads]
    if prompt:
        ct_ref[...] = c.T[0:nheads, :]


def _fox_project(h, an, w_qkv, wf, bf, qg, kg, seg, exp, tri, *, period, prompt):
    n, d = h.shape
    nheads = d // GROUP
    tm = min(PROJ_ROWS, n)
    assert n % tm == 0 and period & (period - 1) == 0 and (period % tm == 0 or tm % period == 0)
    row_f32 = jax.ShapeDtypeStruct((n, d), F32)
    row_bf16 = jax.ShapeDtypeStruct((n, d), BF16)
    head_f32 = jax.ShapeDtypeStruct((n, nheads), F32)
    out_shape = [row_bf16 if prompt else row_f32, row_f32, row_f32, head_f32, head_f32]
    out_specs = [_rows(tm, d), _rows(tm, d), _rows(tm, d), _rows(tm, nheads), _rows(tm, nheads)]
    if prompt:
        out_shape += [row_bf16, row_bf16, jax.ShapeDtypeStruct((nheads, n), F32)]
        out_specs += [_rows(tm, d), _rows(tm, d), pl.BlockSpec((nheads, tm), lambda i: (0, i))]
    return pl.pallas_call(
        functools.partial(_fox_proj_body, tm=tm, d=d, nheads=nheads, period=period, prompt=prompt),
        grid=(n // tm,),
        in_specs=[_rows(tm, d), _full(an.shape), _full(w_qkv.shape), _full(wf.shape), _full(bf.shape),
                  _full(qg.shape), _full(kg.shape), _full(seg.shape), _full(exp.shape), _full(tri.shape)],
        out_specs=out_specs,
        out_shape=out_shape,
        scratch_shapes=[pltpu.VMEM((1, LANES), F32)],
        compiler_params=_params("arbitrary"),
        name="fox_project",
    )(h, an, w_qkv, wf, bf, qg, kg, seg, exp, tri)


def _diff_proj_body(h_ref, an_ref, w_ref, qg_ref, kg_ref, seg_ref, exp_ref, invf_ref, *rest,
                    tm, d, period, offset, prompt):
    if prompt:
        q_ref, kf_ref, vf_ref, kb_ref, vb_ref = rest
    else:
        q_ref, kf_ref, vf_ref = rest
    xb = _rms_rows(h_ref[...], an_ref[...]).astype(BF16)
    tables = _rope_tables(pl.program_id(0), tm, period, offset, invf_ref[...])

    q = jnp.dot(xb, w_ref[:, 0:d], preferred_element_type=F32)
    q = _apply_rope(q * _group_scale(q, seg_ref, exp_ref, GROUP) * qg_ref[...], tables)
    q_ref[...] = (q * GROUP ** -0.5).astype(BF16)
    k = jnp.dot(xb, w_ref[:, d:2 * d], preferred_element_type=F32)
    k = _apply_rope(k * _group_scale(k, seg_ref, exp_ref, GROUP) * kg_ref[...], tables)
    kf_ref[...] = k
    v = jnp.dot(xb, w_ref[:, 2 * d:3 * d], preferred_element_type=F32)
    vf_ref[...] = v
    if prompt:
        kb_ref[...] = k.astype(BF16)
        vb_ref[...] = v.astype(BF16)


def _diff_project(h, an, w_qkv, qg, kg, seg, exp, invf, *, period, offset, prompt):
    n, d = h.shape
    tm = min(PROJ_ROWS, n)
    assert n % tm == 0 and period & (period - 1) == 0
    row_f32 = jax.ShapeDtypeStruct((n, d), F32)
    row_bf16 = jax.ShapeDtypeStruct((n, d), BF16)
    n_out = 5 if prompt else 3
    return pl.pallas_call(
        functools.partial(_diff_proj_body, tm=tm, d=d, period=period, offset=offset, prompt=prompt),
        grid=(n // tm,),
        in_specs=[_rows(tm, d), _full(an.shape), _full(w_qkv.shape), _full(qg.shape), _full(kg.shape),
                  _full(seg.shape), _full(exp.shape), _full(invf.shape)],
        out_specs=[_rows(tm, d)] * n_out,
        out_shape=[row_bf16, row_f32, row_f32, row_bf16, row_bf16][:n_out],
        compiler_params=_params("parallel"),
        name="diff_project",
    )(h, an, w_qkv, qg, kg, seg, exp, invf)


def _diff_lambda(lam_ref, lam_init):
    lp = lam_ref[...]
    a = jnp.sum(lp[0:1, :] * lp[1:2, :], axis=1, keepdims=True)
    b = jnp.sum(lp[2:3, :] * lp[3:4, :], axis=1, keepdims=True)
    return jnp.exp(a) - jnp.exp(b) + lam_init


def _prompt_attn_body(*refs, fox, tq, lam_init):
    if fox:
        q_ref, k_ref, v_ref, c_ref, ct_ref, o_ref = refs
    else:
        q_ref, k_ref, v_ref, lam_ref, o_ref = refs
    col = pl.program_id(1)
    qi = pl.program_id(2)
    q2 = q_ref[...].astype(F32)
    lo = lax.broadcasted_iota(jnp.int32, (tq, LANES), 1) < GROUP
    qs = (jnp.where(lo, q2, 0.0).astype(BF16), jnp.where(lo, 0.0, q2).astype(BF16))
    if fox:
        c_tile = c_ref[...]
        head = lax.broadcasted_iota(jnp.int32, c_tile.shape, 1)
        cq = tuple(jnp.sum(jnp.where(head == 2 * col + s, c_tile, 0.0), axis=1, keepdims=True) for s in (0, 1))
    causal = (lax.broadcasted_iota(jnp.int32, (tq, tq), 0) >= lax.broadcasted_iota(jnp.int32, (tq, tq), 1))

    def tile(kj, carry, diagonal):
        start = pl.multiple_of(kj * tq, tq)
        kt = k_ref[pl.ds(start, tq), :]
        vt = v_ref[pl.ds(start, tq), :]
        new = []
        for s in (0, 1):
            m, l, acc = carry[s]
            sc = lax.dot_general(qs[s], kt, (((1,), (1,)), ((), ())), preferred_element_type=F32)
            if fox:
                sc = sc + (cq[s] - ct_ref[pl.ds(2 * col + s, 1), pl.ds(start, tq)])
            if diagonal:
                sc = jnp.where(causal, sc, -jnp.inf)
            m_new = jnp.maximum(m, jnp.max(sc, axis=1, keepdims=True))
            alpha = jnp.exp(m - m_new)
            p = jnp.exp(sc - m_new)
            l_new = alpha * l + jnp.sum(p, axis=1, keepdims=True)
            acc_new = alpha * acc + jnp.dot(p.astype(BF16), vt, preferred_element_type=F32)
            new.append((m_new, l_new, acc_new))
        return tuple(new)

    init = tuple((jnp.full((tq, 1), -jnp.inf, F32), jnp.zeros((tq, 1), F32), jnp.zeros((tq, LANES), F32))
                 for _ in (0, 1))
    carry = lax.fori_loop(0, qi, lambda kj, c: tile(kj, c, False), init)
    (_, l0, a0), (_, l1, a1) = tile(qi, carry, True)
    if fox:
        o_ref[...] = jnp.where(lo, a0 / l0, a1 / l1).astype(o_ref.dtype)
    else:
        o_ref[...] = (a0 / l0 - _diff_lambda(lam_ref, lam_init) * (a1 / l1)).astype(o_ref.dtype)


def _prompt_attention(q, k, v, extra, *, fox, lam_init=0.0):
    b, s, d = q.shape
    tq = min(ATTN_ROWS, s)
    assert s % tq == 0 and d % LANES == 0
    q_spec = pl.BlockSpec((None, tq, LANES), lambda bi, ci, qi: (bi, qi, ci))
    kv_spec = pl.BlockSpec((None, s, LANES), lambda bi, ci, qi: (bi, 0, ci))
    if fox:
        c, ct = extra
        nheads = c.shape[-1]
        extra_specs = [pl.BlockSpec((None, tq, nheads), lambda bi, ci, qi: (bi, qi, 0)),
                       pl.BlockSpec((nheads, s), lambda bi, ci, qi: (0, bi))]
    else:
        extra_specs = [_full(extra[0].shape)]
    return pl.pallas_call(
        functools.partial(_prompt_attn_body, fox=fox, tq=tq, lam_init=lam_init),
        grid=(b, d // LANES, s // tq),
        in_specs=[q_spec, kv_spec, kv_spec] + extra_specs,
        out_specs=q_spec,
        out_shape=jax.ShapeDtypeStruct((b, s, d), BF16 if fox else F32),
        compiler_params=_params("parallel", "parallel", "arbitrary"),
        name="fox_prompt_attention" if fox else "diff_prompt_attention",
    )(q, k, v, *extra)


def _decode_attn_body(pt_ref, *refs, fox, pages, t_new, ngroups, lam_init):
    del pt_ref
    q_ref, kn_ref, vn_ref, x_ref = refs[:4]
    k_refs = refs[4:4 + pages]
    v_refs = refs[4 + pages:4 + 2 * pages]
    pos = 4 + 2 * pages
    if fox:
        lf_refs = refs[pos:pos + pages]
        u_ref = refs[pos + pages]
        pos += pages + 1
    o_ref, qbd_ref, m_ref, l_ref, acc_ref = refs[pos:pos + 5]
    if fox:
        carry_ref = refs[pos + 5]
    step = pl.program_id(1)
    rows = t_new * ngroups
    d = q_ref.shape[-1]
    own = (_div_pow2(lax.broadcasted_iota(jnp.int32, (ngroups, d), 1), GROUP)
           == lax.broadcasted_iota(jnp.int32, (ngroups, d), 0))

    @pl.when(step == 0)
    def _():
        q = q_ref[...].astype(F32)
        for t in range(t_new):
            row = jnp.where(own, jnp.broadcast_to(q[t:t + 1, :], (ngroups, d)), 0.0)
            if fox:
                hi, lo = _split2(row)
                qbd_ref[t * ngroups:(t + 1) * ngroups, :] = hi
                qbd_ref[rows + t * ngroups:rows + (t + 1) * ngroups, :] = lo
            else:
                qbd_ref[t * ngroups:(t + 1) * ngroups, :] = row.astype(BF16)
        m_ref[...] = jnp.full_like(m_ref, -jnp.inf)
        l_ref[...] = jnp.zeros_like(l_ref)
        acc_ref[...] = jnp.zeros_like(acc_ref)
        if fox:
            carry_ref[...] = jnp.zeros_like(carry_ref)

    def update(sc, pv):
        m = m_ref[...]
        m_new = jnp.maximum(m, jnp.max(sc, axis=1, keepdims=True))
        alpha = jnp.exp(m - m_new)
        p = jnp.exp(sc - m_new)
        l_ref[...] = alpha * l_ref[...] + jnp.sum(p, axis=1, keepdims=True)
        acc_ref[...] = alpha * acc_ref[...] + pv(p)
        m_ref[...] = m_new

    def fold(x):
        return x[0:rows] + x[rows:2 * rows]

    qbd = qbd_ref[...]
    for g in range(pages):
        if fox:
            k_hi, k_lo = _split2(k_refs[g][...])
            sc = (fold(jnp.dot(qbd, k_hi, preferred_element_type=F32))
                  + jnp.dot(qbd[0:rows], k_lo, preferred_element_type=F32))
            u = u_ref[...]
            suf = sum(jnp.dot(p, u, preferred_element_type=F32) for p in _split3(lf_refs[g][...]))
            page = u.shape[0]
            decay = suf[:, 0:page] + carry_ref[...]
            carry_ref[...] = carry_ref[...] + suf[:, page:page + 1]
            sc = sc + (jnp.concatenate([decay] * t_new, axis=0) + x_ref[...])
            v_hi, v_lo = _split2(v_refs[g][...])
            by_key = (((1,), (1,)), ((), ()))

            def pv(p, v_hi=v_hi, v_lo=v_lo):
                p_hi, p_lo = _split2(p)
                both = jnp.concatenate([p_hi, p_lo], axis=0)
                return (fold(lax.dot_general(both, v_hi, by_key, preferred_element_type=F32))
                        + lax.dot_general(p_hi, v_lo, by_key, preferred_element_type=F32))

            update(sc, pv)
        else:
            sc = jnp.dot(qbd, k_refs[g][...].astype(BF16), preferred_element_type=F32)
            v_ref = v_refs[g]
            update(sc, lambda p, v_ref=v_ref: jnp.concatenate(
                [jnp.dot(p.astype(BF16), v_ref[:, h, :].astype(BF16), preferred_element_type=F32)
                 for h in range(v_ref.shape[1])], axis=1))

    @pl.when(step == pl.num_programs(1) - 1)
    def _():
        qf = fold(qbd.astype(F32)) if fox else qbd.astype(F32)
        tok = _div_pow2(lax.broadcasted_iota(jnp.int32, (rows, 1), 0), ngroups)
        for t in range(t_new):
            sc = jnp.sum(qf * kn_ref[t:t + 1, :], axis=1, keepdims=True)
            if fox:
                cn = x_ref[...]
                sc = sc + (cn - jnp.concatenate([cn[t * ngroups:(t + 1) * ngroups, :]] * t_new, axis=0))
            sc = jnp.where(tok >= t, sc, -jnp.inf)
            update(sc, lambda p, t=t: p * vn_ref[t:t + 1, :])
        if fox:
            coef = 1.0 / l_ref[...]
            mine = own
        else:
            sub = jnp.bitwise_and(lax.broadcasted_iota(jnp.int32, (rows, 1), 0), 1)
            coef = jnp.where(sub == 0, 1.0, -_diff_lambda(x_ref, lam_init)) / l_ref[...]
            mine = (_div_pow2(lax.broadcasted_iota(jnp.int32, (ngroups, d), 1), 2 * GROUP)
                    == _div_pow2(lax.broadcasted_iota(jnp.int32, (ngroups, d), 0), 2))
        weighted = acc_ref[...] * coef
        for t in range(t_new):
            blk = jnp.where(mine, weighted[t * ngroups:(t + 1) * ngroups, :], 0.0)
            o_ref[t:t + 1, :] = jnp.sum(blk, axis=0, keepdims=True)


def _decode_attention(page_table, q, k_new, v_new, extra, k_cache, v_cache, lf_cache=None, *, fox, lam_init=0.0):
    nb, t_new, d = q.shape
    ngroups = d // GROUP
    rows = t_new * ngroups
    n_pages = page_table.shape[1]
    page = k_cache.shape[2]
    pages = math.gcd(DECODE_PAGES_PER_STEP, n_pages)
    steps = n_pages // pages

    def tok_spec():
        return pl.BlockSpec((None, t_new, d), lambda b, s, pt: (b, 0, 0))

    def page_spec(g, shape):
        zeros = (0,) * len(shape)
        return pl.BlockSpec((None,) + shape, lambda b, s, pt: (pt[b, n_pages - 1 - (s * pages + g)],) + zeros)

    if fox:
        x_spec = pl.BlockSpec((None, rows, 1), lambda b, s, pt: (b, 0, 0))
    else:
        x_spec = pl.BlockSpec(extra.shape, lambda b, s, pt: (0, 0))
    in_specs = [tok_spec(), tok_spec(), tok_spec(), x_spec]
    in_specs += [page_spec(g, k_cache.shape[1:]) for g in range(pages)]
    in_specs += [page_spec(g, v_cache.shape[1:]) for g in range(pages)]
    args = [q, k_new, v_new, extra] + [k_cache] * pages + [v_cache] * pages
    scratch = [pltpu.VMEM(((2 if fox else 1) * rows, d), BF16), pltpu.VMEM((rows, 1), F32),
               pltpu.VMEM((rows, 1), F32), pltpu.VMEM((rows, d), F32)]
    if fox:
        j = lax.broadcasted_iota(jnp.int32, (page, MXU_COLS), 0)
        i = lax.broadcasted_iota(jnp.int32, (page, MXU_COLS), 1)
        u = jnp.where((j > i) | (i >= page), 1.0, 0.0).astype(BF16)
        in_specs += [page_spec(g, lf_cache.shape[1:]) for g in range(pages)]
        in_specs += [pl.BlockSpec(u.shape, lambda b, s, pt: (0, 0))]
        args += [lf_cache] * pages + [u]
        scratch += [pltpu.VMEM((ngroups, 1), F32)]
    return pl.pallas_call(
        functools.partial(_decode_attn_body, fox=fox, pages=pages, t_new=t_new, ngroups=ngroups, lam_init=lam_init),
        grid_spec=pltpu.PrefetchScalarGridSpec(
            num_scalar_prefetch=1, grid=(nb, steps), in_specs=in_specs, out_specs=tok_spec(),
            scratch_shapes=scratch),
        out_shape=jax.ShapeDtypeStruct((nb, t_new, d), F32),
        compiler_params=_params("parallel", "arbitrary"),
        name="fox_decode_attention" if fox else "diff_decode_attention",
    )(page_table, *args)


def _out_proj_body(*refs, diff, out_scale):
    if diff:
        h_ref, o_ref, w_ref, g_ref, seg_ref, exp_ref, y_ref = refs
        o = o_ref[...].astype(F32)
        o = o * _group_scale(o, seg_ref, exp_ref, 2 * GROUP) * g_ref[...] * out_scale
    else:
        h_ref, o_ref, w_ref, y_ref = refs
        o = o_ref[...]
    pieces = _split2(o) if w_ref.shape[0] == 2 else (o.astype(BF16),)
    y_ref[...] = h_ref[...] + _dot_pieces(pieces, w_ref)


def _out_project(h, o, w, *norm, diff, out_scale=1.0):
    n, d = h.shape
    tm = min(PROJ_ROWS, n)
    assert n % tm == 0
    return pl.pallas_call(
        functools.partial(_out_proj_body, diff=diff, out_scale=out_scale),
        grid=(n // tm,),
        in_specs=[_rows(tm, d), _rows(tm, d), _full(w.shape)] + [_full(a.shape) for a in norm],
        out_specs=_rows(tm, d),
        out_shape=jax.ShapeDtypeStruct((n, d), F32),
        compiler_params=_params("parallel"),
        name="attn_out_project",
    )(h, o, w, *norm)


def _route(logits, n_experts, n_groups):
    per_group = n_experts // n_groups
    lane = lax.broadcasted_iota(jnp.int32, logits.shape, 1)
    neg = -jnp.inf
    big = jnp.int32(LANES)

    def lane_max(x):
        return jnp.max(x, axis=1, keepdims=True)

    def first_lane(mask):
        return jnp.min(jnp.where(mask, lane, big), axis=1, keepdims=True)

    is_group = (lane >= n_experts) & (lane < n_experts + n_groups)
    g_logit = jnp.where(is_group, logits, neg)
    g_max = lane_max(g_logit)
    g_sel = first_lane(g_logit == g_max) - n_experts
    g_weight = 1.0 / jnp.sum(jnp.exp(g_logit - g_max), axis=1, keepdims=True)

    chosen = (lane < n_experts) & (_div_pow2(lane, per_group) == g_sel)
    e_logit = jnp.where(chosen, logits, neg)
    e_exp = jnp.exp(e_logit - lane_max(e_logit))
    e_prob = e_exp / jnp.sum(e_exp, axis=1, keepdims=True)
    w1 = lane_max(e_prob)
    i1 = first_lane(chosen & (e_prob == w1))
    rest = chosen & (lane != i1)
    e_rest = jnp.where(rest, e_prob, -1.0)
    w2 = lane_max(e_rest)
    i2 = first_lane(rest & (e_rest == w2))
    norm = g_weight / (w1 + w2)
    return jnp.where(lane == i1, w1 * norm, 0.0) + jnp.where(lane == i2, w2 * norm, 0.0)


def _moe_body(h_ref, fn_ref, wr_ref, br_ref, wg_ref, wu_ref, wd_ref, y_ref, xn_ref, gate_ref, *,
              n_experts, n_groups):
    e = pl.program_id(1)

    @pl.when(e == 0)
    def _():
        h = h_ref[...]
        xn = _rms_rows(h, fn_ref[...])
        xb = xn.astype(BF16)
        xn_ref[...] = xb
        x_lo = (xn - xb.astype(F32)).astype(BF16)
        logits = (jnp.dot(xb, wr_ref[0], preferred_element_type=F32)
                  + jnp.dot(xb, wr_ref[1], preferred_element_type=F32)
                  + jnp.dot(x_lo, wr_ref[0], preferred_element_type=F32)) + br_ref[...]
        gate_ref[...] = _route(logits, n_experts, n_groups)
        y_ref[...] = h

    xb = xn_ref[...]
    gate = jnp.dot(xb, wg_ref[...], preferred_element_type=F32)
    up = jnp.dot(xb, wu_ref[...], preferred_element_type=F32)
    hid = (gate * jax.nn.sigmoid(gate) * up).astype(BF16)
    out = jnp.dot(hid, wd_ref[...], preferred_element_type=F32)
    lane = lax.broadcasted_iota(jnp.int32, gate_ref.shape, 1)
    w = jnp.sum(jnp.where(lane == e, gate_ref[...], 0.0), axis=1, keepdims=True)
    y_ref[...] += out * w


def _moe(h, fn, wr, br, wg, wu, wd, *, n_groups):
    n, d = h.shape
    n_experts, _, ff = wg.shape
    tm = min(MOE_ROWS, n)
    assert n % tm == 0
    return pl.pallas_call(
        functools.partial(_moe_body, n_experts=n_experts, n_groups=n_groups),
        grid=(n // tm, n_experts),
        in_specs=[pl.BlockSpec((tm, d), lambda i, e: (i, 0)),
                  pl.BlockSpec(fn.shape, lambda i, e: (0, 0)),
                  pl.BlockSpec(wr.shape, lambda i, e: (0, 0, 0)),
                  pl.BlockSpec(br.shape, lambda i, e: (0, 0)),
                  pl.BlockSpec((None, d, ff), lambda i, e: (e, 0, 0)),
                  pl.BlockSpec((None, d, ff), lambda i, e: (e, 0, 0)),
                  pl.BlockSpec((None, ff, d), lambda i, e: (e, 0, 0))],
        out_specs=pl.BlockSpec((tm, d), lambda i, e: (i, 0)),
        out_shape=jax.ShapeDtypeStruct((n, d), F32),
        scratch_shapes=[pltpu.VMEM((tm, d), BF16), pltpu.VMEM((tm, LANES), F32)],
        compiler_params=_params("parallel", "arbitrary"),
        name="hier_moe",
    )(h, fn, wr, br, wg, wu, wd)


def _ple_body(h_ref, p_ref, g_ref, wg_ref, wp_ref, y_ref):
    h = h_ref[...]
    xb = _rms_rows(h, g_ref[...]).astype(BF16)
    gate = jax.nn.sigmoid(jnp.dot(xb, wg_ref[...], preferred_element_type=F32))
    proj = jnp.dot(p_ref[...].astype(BF16), wp_ref[...], preferred_element_type=F32)
    y_ref[...] = h + gate * proj


def _ple(h, p, g, wg, wp):
    n, d = h.shape
    tm = min(PROJ_ROWS, n)
    assert n % tm == 0
    return pl.pallas_call(
        _ple_body,
        grid=(n // tm,),
        in_specs=[_rows(tm, d), _rows(tm, p.shape[1]), _full(g.shape), _full(wg.shape), _full(wp.shape)],
        out_specs=_rows(tm, d),
        out_shape=jax.ShapeDtypeStruct((n, d), F32),
        compiler_params=_params("parallel"),
        name="per_layer_embedding",
    )(h, p, g, wg, wp)


def _segment_tables(d, group):
    lane_group = jnp.arange(d)[:, None] // group
    seg = (lane_group == jnp.arange(LANES)[None, :]).astype(BF16)
    return seg, jnp.concatenate([seg.T, seg.T], axis=0)


def _cumsum_table(tm, period):
    r = jnp.arange(tm)[:, None]
    s = jnp.arange(tm)[None, :]
    return ((s <= r) & (r // period == s // period)).astype(BF16)


def _row(vec, reps=1):
    return jnp.tile(vec.astype(F32), reps)[None, :]


def _hi_lo(w):
    w = w.astype(F32)
    hi = w.astype(BF16)
    return jnp.stack([hi, (w - hi.astype(F32)).astype(BF16)])


def _pad_lanes(x):
    return jnp.pad(x, [(0, 0)] * (x.ndim - 1) + [(0, LANES - x.shape[-1])])


def kernel(x_prompt, x_sample, p_prompt, p_sample, cache_fox_k, cache_fox_v, cache_fox_logf, cache_diff_k, cache_diff_v, page_table, attn_norm, fox_w_in, fox_b_f, fox_q_norm, fox_k_norm, fox_w_out, diff_w_in, diff_q_norm, diff_k_norm, diff_lambda, diff_subln, diff_w_out, ffn_norm, moe_w_group, moe_b_group, moe_w_expert, moe_b_expert, moe_w_gate, moe_w_up, moe_w_down, ple_norm, ple_w_gate, ple_w_proj):
    b, s, d = x_prompt.shape
    nb, t_new, _ = x_sample.shape
    depth = attn_norm.shape[0]
    n_pool, page = cache_fox_k.shape[1:3]
    past_len = page_table.shape[1] * page
    fox_heads = fox_b_f.shape[-1]
    diff_heads = d // (2 * GROUP)
    n_groups = moe_w_group.shape[-1]
    n_experts = moe_w_expert.shape[-1]
    assert fox_q_norm.shape[-1] == GROUP and diff_q_norm.shape[-1] == GROUP and fox_heads * GROUP == d
    assert 2 * fox_heads <= LANES and n_experts + n_groups <= LANES

    seg64, exp64 = _segment_tables(d, GROUP)
    seg128, exp128 = _segment_tables(d, 2 * GROUP)
    tri_p = _cumsum_table(min(PROJ_ROWS, b * s), s)
    tri_s = _cumsum_table(min(PROJ_ROWS, nb * t_new), t_new)
    rot = GROUP // ROPE_FRACTION
    inv_freq = jnp.power(jnp.float32(ROPE_THETA), -jnp.arange(rot // 2, dtype=jnp.float32) * (2.0 / rot))
    lane_d = jnp.arange(LANES) % GROUP
    invf = jnp.where(lane_d < rot, inv_freq[lane_d % (rot // 2)], 0.0)[None, :]

    fk_p, fv_p, ff_p, dk_p, dv_p = [], [], [], [], []
    fk_s, fv_s, ff_s, dk_s, dv_s = [], [], [], [], []
    hp = x_prompt.reshape(b * s, d)
    hs = x_sample.reshape(nb * t_new, d)
    for i in range(depth):
        j = i // N_MIXERS
        an = _row(attn_norm[i])
        if i % N_MIXERS == 0:
            w_qkv = _hi_lo(fox_w_in[j][:, :3 * d])
            wf = _pad_lanes(jnp.concatenate(list(_hi_lo(fox_w_in[j][:, 3 * d:])), axis=1))
            bf = _pad_lanes(_row(fox_b_f[j]))
            qg, kg = _row(fox_q_norm[j], fox_heads), _row(fox_k_norm[j], fox_heads)
            w_out = _hi_lo(fox_w_out[j])
            qp, kp, vp, lfp, cp, kbp, vbp, ctp = _fox_project(
                hp, an, w_qkv[:1], wf, bf, qg, kg, seg64, exp64, tri_p, period=s, prompt=True)
            qs, ks, vs, lfs, cs = _fox_project(
                hs, an, w_qkv, wf, bf, qg, kg, seg64, exp64, tri_s, period=t_new, prompt=False)
            op = _prompt_attention(qp.reshape(b, s, d), kbp.reshape(b, s, d), vbp.reshape(b, s, d),
                                   (cp.reshape(b, s, fox_heads), ctp), fox=True)
            os_ = _decode_attention(
                page_table, qs.reshape(nb, t_new, d), ks.reshape(nb, t_new, d), vs.reshape(nb, t_new, d),
                cs.reshape(nb, t_new * fox_heads, 1),
                jnp.transpose(cache_fox_k[j], (0, 2, 3, 1)).reshape(n_pool, d, page),
                jnp.transpose(cache_fox_v[j], (0, 2, 3, 1)).reshape(n_pool, d, page),
                jnp.transpose(cache_fox_logf[j], (0, 2, 1)), fox=True)
            hp = _out_project(hp, op.reshape(b * s, d), w_out[:1], diff=False)
            hs = _out_project(hs, os_.reshape(nb * t_new, d), w_out, diff=False)
            fk_p.append(kp.reshape(b, s, fox_heads, GROUP))
            fv_p.append(vp.reshape(b, s, fox_heads, GROUP))
            ff_p.append(lfp.reshape(b, s, fox_heads))
            fk_s.append(ks.reshape(nb, t_new, fox_heads, GROUP))
            fv_s.append(vs.reshape(nb, t_new, fox_heads, GROUP))
            ff_s.append(lfs.reshape(nb, t_new, fox_heads))
        else:
            lam_init = 0.8 - 0.6 * math.exp(-0.3 * i)
            w_qkv = diff_w_in[j].astype(BF16)
            qg, kg = _row(diff_q_norm[j], 2 * diff_heads), _row(diff_k_norm[j], 2 * diff_heads)
            w_out = diff_w_out[j].astype(BF16)[None]
            subln = _row(diff_subln[j], diff_heads)
            lam = diff_lambda[j].astype(F32)
            qp, kp, vp, kbp, vbp = _diff_project(hp, an, w_qkv, qg, kg, seg64, exp64, invf,
                                                 period=s, offset=0, prompt=True)
            qs, ks, vs = _diff_project(hs, an, w_qkv, qg, kg, seg64, exp64, invf,
                                       period=t_new, offset=past_len, prompt=False)
            op = _prompt_attention(qp.reshape(b, s, d), kbp.reshape(b, s, d), vbp.reshape(b, s, d),
                                   (lam,), fox=False, lam_init=lam_init)
            os_ = _decode_attention(
                page_table, qs.reshape(nb, t_new, d), ks.reshape(nb, t_new, d), vs.reshape(nb, t_new, d),
                lam, jnp.transpose(cache_diff_k[j], (0, 2, 3, 4, 1)).reshape(n_pool, d, page), cache_diff_v[j],
                fox=False, lam_init=lam_init)
            hp = _out_project(hp, op.reshape(b * s, d), w_out, subln, seg128, exp128,
                              diff=True, out_scale=1.0 - lam_init)
            hs = _out_project(hs, os_.reshape(nb * t_new, d), w_out, subln, seg128, exp128,
                              diff=True, out_scale=1.0 - lam_init)
            dk_p.append(kp.reshape(b, s, diff_heads, 2, GROUP))
            dv_p.append(vp.reshape(b, s, diff_heads, 2 * GROUP))
            dk_s.append(ks.reshape(nb, t_new, diff_heads, 2, GROUP))
            dv_s.append(vs.reshape(nb, t_new, diff_heads, 2 * GROUP))

        wr = _hi_lo(_pad_lanes(jnp.concatenate([moe_w_expert[i], moe_w_group[i]], axis=1)))
        br = _pad_lanes(_row(jnp.concatenate([moe_b_expert[i], moe_b_group[i]])))
        moe_args = (_row(ffn_norm[i]), wr, br, moe_w_gate[i].astype(BF16), moe_w_up[i].astype(BF16),
                    moe_w_down[i].astype(BF16))
        hp = _moe(hp, *moe_args, n_groups=n_groups)
        hs = _moe(hs, *moe_args, n_groups=n_groups)

        ple_args = (_row(ple_norm[i]), ple_w_gate[i].astype(BF16), ple_w_proj[i].astype(BF16))
        hp = _ple(hp, p_prompt[i].reshape(b * s, -1), *ple_args)
        hs = _ple(hs, p_sample[i].reshape(nb * t_new, -1), *ple_args)

    return (hp.reshape(b, s, d), hs.reshape(nb, t_new, d),
            jnp.stack(fk_p), jnp.stack(fv_p), jnp.stack(ff_p), jnp.stack(dk_p), jnp.stack(dv_p),
            jnp.stack(fk_s), jnp.stack(fv_s), jnp.stack(ff_s), jnp.stack(dk_s), jnp.stack(dv_s))
```

```python
import functools
import math

import jax
import jax.numpy as jnp
from jax import lax
from jax.experimental import pallas as pl
from jax.experimental.pallas import tpu as pltpu

F32 = jnp.float32
BF16 = jnp.bfloat16

LANES = 128
MXU_COLS = 256
GROUP = 64
VMEM_LIMIT_BYTES = 56 * 1024 * 1024
RMS_EPS = 1e-6
ROPE_THETA = 500000.0
ROPE_FRACTION = 4
N_MIXERS = 2
TOP_K = 2

PROJ_ROWS = 512
ATTN_ROWS = 512
MOE_ROWS = 512
DECODE_PAGES_PER_STEP = 4


def _params(*sem):
    return pltpu.CompilerParams(dimension_semantics=sem, vmem_limit_bytes=VMEM_LIMIT_BYTES)


def _full(shape):
    zeros = (0,) * len(shape)
    return pl.BlockSpec(shape, lambda *_: zeros)


def _rows(tm, width):
    return pl.BlockSpec((tm, width), lambda i: (i, 0))


def _rms_rows(x, gain):
    return x * lax.rsqrt(jnp.mean(x * x, axis=-1, keepdims=True) + RMS_EPS) * gain


def _div_pow2(x, n):
    assert n & (n - 1) == 0
    return jnp.right_shift(x, n.bit_length() - 1)


def _split3(x):
    p1 = x.astype(BF16)
    r1 = x - p1.astype(F32)
    p2 = r1.astype(BF16)
    p3 = (r1 - p2.astype(F32)).astype(BF16)
    return p1, p2, p3


def _split2(x):
    hi = x.astype(BF16)
    return hi, (x - hi.astype(F32)).astype(BF16)


def _dot_pieces(xs, w_ref, cols=slice(None)):
    out = jnp.dot(xs[0], w_ref[0, :, cols], preferred_element_type=F32)
    if w_ref.shape[0] == 2:
        out = (out + jnp.dot(xs[1], w_ref[0, :, cols], preferred_element_type=F32)
               + jnp.dot(xs[0], w_ref[1, :, cols], preferred_element_type=F32))
    return out


def _group_scale(x, seg_ref, exp_ref, group):
    sq_hi, sq_lo = _split2(x * x)
    ss = (jnp.dot(sq_hi, seg_ref[...], preferred_element_type=F32)
          + jnp.dot(sq_lo, seg_ref[...], preferred_element_type=F32))
    hi, lo = _split2(lax.rsqrt(ss * (1.0 / group) + RMS_EPS))
    return jnp.dot(jnp.concatenate([hi, lo], axis=1), exp_ref[...], preferred_element_type=F32)


def _rope_tables(tile, tm, period, offset, invf):
    half = GROUP // ROPE_FRACTION // 2
    row = lax.broadcasted_iota(jnp.int32, (tm, LANES), 0) + tile * tm
    pos = (jnp.bitwise_and(row, period - 1) + offset).astype(F32)
    ang = pos * invf
    cs = jnp.cos(ang)
    sn = jnp.sin(ang)
    d = jnp.bitwise_and(lax.broadcasted_iota(jnp.int32, (tm, LANES), 1), GROUP - 1)
    return cs, jnp.where(d < half, -sn, 0.0), jnp.where(d >= half, sn, 0.0)


def _apply_rope(x, tables):
    cs, s_lo, s_hi = tables
    half = GROUP // ROPE_FRACTION // 2
    out = []
    for j in range(x.shape[1] // LANES):
        xs = x[:, j * LANES:(j + 1) * LANES]
        out.append(xs * cs + pltpu.roll(xs, LANES - half, 1) * s_lo + pltpu.roll(xs, half, 1) * s_hi)
    return jnp.concatenate(out, axis=1)


def _fox_proj_body(h_ref, an_ref, w_ref, wf_ref, bf_ref, qg_ref, kg_ref, seg_ref, exp_ref, tri_ref,
                   *rest, tm, d, nheads, period, prompt):
    if prompt:
        q_ref, kf_ref, vf_ref, lf_ref, c_ref, kb_ref, vb_ref, ct_ref, carry_ref = rest
    else:
        q_ref, kf_ref, vf_ref, lf_ref, c_ref, carry_ref = rest
    tile = pl.program_id(0)
    xn = _rms_rows(h_ref[...], an_ref[...])
    xb, x_lo = xs = _split2(xn)

    q = _dot_pieces(xs, w_ref, slice(0, d))
    q = q * _group_scale(q, seg_ref, exp_ref, GROUP) * qg_ref[...]
    q_ref[...] = (q * GROUP ** -0.5).astype(q_ref.dtype)
    k = _dot_pieces(xs, w_ref, slice(d, 2 * d))
    k = k * _group_scale(k, seg_ref, exp_ref, GROUP) * kg_ref[...]
    kf_ref[...] = k
    v = _dot_pieces(xs, w_ref, slice(2 * d, 3 * d))
    vf_ref[...] = v
    if prompt:
        kb_ref[...] = k.astype(BF16)
        vb_ref[...] = v.astype(BF16)

    z_hi = jnp.dot(xb, wf_ref[...], preferred_element_type=F32)
    z_lo = jnp.dot(x_lo, wf_ref[...], preferred_element_type=F32)
    z = z_hi + pltpu.roll(z_hi, LANES - nheads, 1) + z_lo + bf_ref[...]
    lf = jnp.minimum(z, 0.0) - jnp.log1p(jnp.exp(-jnp.abs(z)))
    lane = lax.broadcasted_iota(jnp.int32, (tm, LANES), 1)
    lf = jnp.where(lane < nheads, lf, 0.0)
    lf_ref[...] = lf[:, 0:nheads]

    p1, p2, p3 = _split3(lf)
    tri = tri_ref[...]
    c = (jnp.dot(tri, p1, preferred_element_type=F32) + jnp.dot(tri, p2, preferred_element_type=F32)
         + jnp.dot(tri, p3, preferred_element_type=F32))
    if period > tm:
        @pl.when(jnp.bitwise_and(tile * tm, period - 1) == 0)
        def _():
            carry_ref[...] = jnp.zeros_like(carry_ref)
        c = c + carry_ref[...]
        carry_ref[...] = c[tm - 1:tm, :]
    c_ref[...] = c[:, 0:nheads]
    if prompt:
        ct_ref[...] = c.T[0:nheads, :]


def _fox_project(h, an, w_qkv, wf, bf, qg, kg, seg, exp, tri, *, period, prompt):
    n, d = h.shape
    nheads = d // GROUP
    tm = min(PROJ_ROWS, n)
    assert n % tm == 0 and period & (period - 1) == 0 and (period % tm == 0 or tm % period == 0)
    row_f32 = jax.ShapeDtypeStruct((n, d), F32)
    row_bf16 = jax.ShapeDtypeStruct((n, d), BF16)
    head_f32 = jax.ShapeDtypeStruct((n, nheads), F32)
    out_shape = [row_bf16 if prompt else row_f32, row_f32, row_f32, head_f32, head_f32]
    out_specs = [_rows(tm, d), _rows(tm, d), _rows(tm, d), _rows(tm, nheads), _rows(tm, nheads)]
    if prompt:
        out_shape += [row_bf16, row_bf16, jax.ShapeDtypeStruct((nheads, n), F32)]
        out_specs += [_rows(tm, d), _rows(tm, d), pl.BlockSpec((nheads, tm), lambda i: (0, i))]
    return pl.pallas_call(
        functools.partial(_fox_proj_body, tm=tm, d=d, nheads=nheads, period=period, prompt=prompt),
        grid=(n // tm,),
        in_specs=[_rows(tm, d), _full(an.shape), _full(w_qkv.shape), _full(wf.shape), _full(bf.shape),
                  _full(qg.shape), _full(kg.shape), _full(seg.shape), _full(exp.shape), _full(tri.shape)],
        out_specs=out_specs,
        out_shape=out_shape,
        scratch_shapes=[pltpu.VMEM((1, LANES), F32)],
        compiler_params=_params("arbitrary"),
        name="fox_project",
    )(h, an, w_qkv, wf, bf, qg, kg, seg, exp, tri)


def _diff_proj_body(h_ref, an_ref, w_ref, qg_ref, kg_ref, seg_ref, exp_ref, invf_ref, *rest,
                    tm, d, period, offset, prompt):
    if prompt:
        q_ref, kf_ref, vf_ref, kb_ref, vb_ref = rest
    else:
        q_ref, kf_ref, vf_ref = rest
    xb = _rms_rows(h_ref[...], an_ref[...]).astype(BF16)
    tables = _rope_tables(pl.program_id(0), tm, period, offset, invf_ref[...])

    q = jnp.dot(xb, w_ref[:, 0:d], preferred_element_type=F32)
    q = _apply_rope(q * _group_scale(q, seg_ref, exp_ref, GROUP) * qg_ref[...], tables)
    q_ref[...] = (q * GROUP ** -0.5).astype(BF16)
    k = jnp.dot(xb, w_ref[:, d:2 * d], preferred_element_type=F32)
    k = _apply_rope(k * _group_scale(k, seg_ref, exp_ref, GROUP) * kg_ref[...], tables)
    kf_ref[...] = k
    v = jnp.dot(xb, w_ref[:, 2 * d:3 * d], preferred_element_type=F32)
    vf_ref[...] = v
    if prompt:
        kb_ref[...] = k.astype(BF16)
        vb_ref[...] = v.astype(BF16)


def _diff_project(h, an, w_qkv, qg, kg, seg, exp, invf, *, period, offset, prompt):
    n, d = h.shape
    tm = min(PROJ_ROWS, n)
    assert n % tm == 0 and period & (period - 1) == 0
    row_f32 = jax.ShapeDtypeStruct((n, d), F32)
    row_bf16 = jax.ShapeDtypeStruct((n, d), BF16)
    n_out = 5 if prompt else 3
    return pl.pallas_call(
        functools.partial(_diff_proj_body, tm=tm, d=d, period=period, offset=offset, prompt=prompt),
        grid=(n // tm,),
        in_specs=[_rows(tm, d), _full(an.shape), _full(w_qkv.shape), _full(qg.shape), _full(kg.shape),
                  _full(seg.shape), _full(exp.shape), _full(invf.shape)],
        out_specs=[_rows(tm, d)] * n_out,
        out_shape=[row_bf16, row_f32, row_f32, row_bf16, row_bf16][:n_out],
        compiler_params=_params("parallel"),
        name="diff_project",
    )(h, an, w_qkv, qg, kg, seg, exp, invf)


def _diff_lambda(lam_ref, lam_init):
    lp = lam_ref[...]
    a = jnp.sum(lp[0:1, :] * lp[1:2, :], axis=1, keepdims=True)
    b = jnp.sum(lp[2:3, :] * lp[3:4, :], axis=1, keepdims=True)
    return jnp.exp(a) - jnp.exp(b) + lam_init


def _prompt_attn_body(*refs, fox, tq, lam_init):
    if fox:
        q_ref, k_ref, v_ref, c_ref, ct_ref, o_ref = refs
    else:
        q_ref, k_ref, v_ref, lam_ref, o_ref = refs
    col = pl.program_id(1)
    qi = pl.program_id(2)
    q2 = q_ref[...].astype(F32)
    lo = lax.broadcasted_iota(jnp.int32, (tq, LANES), 1) < GROUP
    qs = (jnp.where(lo, q2, 0.0).astype(BF16), jnp.where(lo, 0.0, q2).astype(BF16))
    if fox:
        c_tile = c_ref[...]
        head = lax.broadcasted_iota(jnp.int32, c_tile.shape, 1)
        cq = tuple(jnp.sum(jnp.where(head == 2 * col + s, c_tile, 0.0), axis=1, keepdims=True) for s in (0, 1))
    causal = (lax.broadcasted_iota(jnp.int32, (tq, tq), 0) >= lax.broadcasted_iota(jnp.int32, (tq, tq), 1))

    def tile(kj, carry, diagonal):
        start = pl.multiple_of(kj * tq, tq)
        kt = k_ref[pl.ds(start, tq), :]
        vt = v_ref[pl.ds(start, tq), :]
        new = []
        for s in (0, 1):
            m, l, acc = carry[s]
            sc = lax.dot_general(qs[s], kt, (((1,), (1,)), ((), ())), preferred_element_type=F32)
            if fox:
                sc = sc + (cq[s] - ct_ref[pl.ds(2 * col + s, 1), pl.ds(start, tq)])
            if diagonal:
                sc = jnp.where(causal, sc, -jnp.inf)
            m_new = jnp.maximum(m, jnp.max(sc, axis=1, keepdims=True))
            alpha = jnp.exp(m - m_new)
            p = jnp.exp(sc - m_new)
            l_new = alpha * l + jnp.sum(p, axis=1, keepdims=True)
            acc_new = alpha * acc + jnp.dot(p.astype(BF16), vt, preferred_element_type=F32)
            new.append((m_new, l_new, acc_new))
        return tuple(new)

    init = tuple((jnp.full((tq, 1), -jnp.inf, F32), jnp.zeros((tq, 1), F32), jnp.zeros((tq, LANES), F32))
                 for _ in (0, 1))
    carry = lax.fori_loop(0, qi, lambda kj, c: tile(kj, c, False), init)
    (_, l0, a0), (_, l1, a1) = tile(qi, carry, True)
    if fox:
        o_ref[...] = jnp.where(lo, a0 / l0, a1 / l1).astype(o_ref.dtype)
    else:
        o_ref[...] = (a0 / l0 - _diff_lambda(lam_ref, lam_init) * (a1 / l1)).astype(o_ref.dtype)


def _prompt_attention(q, k, v, extra, *, fox, lam_init=0.0):
    b, s, d = q.shape
    tq = min(ATTN_ROWS, s)
    assert s % tq == 0 and d % LANES == 0
    q_spec = pl.BlockSpec((None, tq, LANES), lambda bi, ci, qi: (bi, qi, ci))
    kv_spec = pl.BlockSpec((None, s, LANES), lambda bi, ci, qi: (bi, 0, ci))
    if fox:
        c, ct = extra
        nheads = c.shape[-1]
        extra_specs = [pl.BlockSpec((None, tq, nheads), lambda bi, ci, qi: (bi, qi, 0)),
                       pl.BlockSpec((nheads, s), lambda bi, ci, qi: (0, bi))]
    else:
        extra_specs = [_full(extra[0].shape)]
    return pl.pallas_call(
        functools.partial(_prompt_attn_body, fox=fox, tq=tq, lam_init=lam_init),
        grid=(b, d // LANES, s // tq),
        in_specs=[q_spec, kv_spec, kv_spec] + extra_specs,
        out_specs=q_spec,
        out_shape=jax.ShapeDtypeStruct((b, s, d), BF16 if fox else F32),
        compiler_params=_params("parallel", "parallel", "arbitrary"),
        name="fox_prompt_attention" if fox else "diff_prompt_attention",
    )(q, k, v, *extra)


def _decode_attn_body(pt_ref, *refs, fox, pages, t_new, ngroups, lam_init):
    del pt_ref
    q_ref, kn_ref, vn_ref, x_ref = refs[:4]
    k_refs = refs[4:4 + pages]
    v_refs = refs[4 + pages:4 + 2 * pages]
    pos = 4 + 2 * pages
    if fox:
        lf_refs = refs[pos:pos + pages]
        u_ref = refs[pos + pages]
        pos += pages + 1
    else:
        spread_ref = refs[pos]
        pos += 1
    o_ref, qbd_ref, m_ref, l_ref, acc_ref, last_ref = refs[pos:pos + 6]
    step = pl.program_id(1)
    rows = t_new * ngroups
    d = q_ref.shape[-1]
    own = (_div_pow2(lax.broadcasted_iota(jnp.int32, (ngroups, d), 1), GROUP)
           == lax.broadcasted_iota(jnp.int32, (ngroups, d), 0))

    @pl.when(step == 0)
    def _():
        q = q_ref[...].astype(F32)
        for t in range(t_new):
            row = jnp.where(own, jnp.broadcast_to(q[t:t + 1, :], (ngroups, d)), 0.0)
            if fox:
                hi, lo = _split2(row)
                qbd_ref[t * ngroups:(t + 1) * ngroups, :] = hi
                qbd_ref[rows + t * ngroups:rows + (t + 1) * ngroups, :] = lo
            else:
                qbd_ref[t * ngroups:(t + 1) * ngroups, :] = row.astype(BF16)
        m_ref[...] = jnp.full_like(m_ref, -jnp.inf)
        l_ref[...] = jnp.zeros_like(l_ref)
        acc_ref[...] = jnp.zeros_like(acc_ref)
        last_ref[...] = jnp.zeros_like(last_ref)

    carry_ref = last_ref

    def update(scs, pvs, target, others=()):
        m = m_ref[...]
        m_new = m
        for sc in scs:
            m_new = jnp.maximum(m_new, jnp.max(sc, axis=1, keepdims=True))
        alpha = jnp.exp(m - m_new)
        ps = [jnp.exp(sc - m_new) for sc in scs]
        l_ref[...] = alpha * l_ref[...] + sum(jnp.sum(p, axis=1, keepdims=True) for p in ps)
        target[...] = alpha * target[...] + sum(pv(p) for pv, p in zip(pvs, ps))
        for ref in others:
            ref[...] = alpha * ref[...]
        m_ref[...] = m_new

    def fold(x):
        return x[0:rows] + x[rows:2 * rows]

    qbd = qbd_ref[...]
    if not fox:
        spread = spread_ref[...]
        heads = spread.shape[1] // spread.shape[0]
        own_head = (jnp.bitwise_and(lax.broadcasted_iota(jnp.int32, (rows, spread.shape[1]), 1), heads - 1)
                    == _div_pow2(jnp.bitwise_and(lax.broadcasted_iota(jnp.int32, (rows, spread.shape[1]), 0),
                                                 ngroups - 1), ngroups // heads))
    scs, pvs = [], []
    if fox:
        carry = carry_ref[...]
    for g in range(pages):
        if fox:
            k_hi, k_lo = _split2(k_refs[g][...])
            sc = (fold(jnp.dot(qbd, k_hi, preferred_element_type=F32))
                  + jnp.dot(qbd[0:rows], k_lo, preferred_element_type=F32))
            u = u_ref[...]
            suf = sum(jnp.dot(p, u, preferred_element_type=F32) for p in _split3(lf_refs[g][...]))
            page = u.shape[0]
            decay = suf[:, 0:page] + carry
            carry = carry + suf[:, page:page + 1]
            scs.append(sc + (jnp.concatenate([decay] * t_new, axis=0) + x_ref[...]))
            by_key = (((1,), (1,)), ((), ()))

            def pv(p, v_ref=v_refs[g]):
                v_hi, v_lo = _split2(v_ref[...])
                p_hi, p_lo = _split2(p)
                both = jnp.concatenate([p_hi, p_lo], axis=0)
                return (fold(lax.dot_general(both, v_hi, by_key, preferred_element_type=F32))
                        + lax.dot_general(p_hi, v_lo, by_key, preferred_element_type=F32))
        else:
            scs.append(jnp.dot(qbd, k_refs[g][...].astype(BF16), preferred_element_type=F32))

            def pv(p, v_ref=v_refs[g]):
                wide = jnp.dot(p.astype(BF16), spread, preferred_element_type=F32)
                return jnp.dot(jnp.where(own_head, wide, 0.0).astype(BF16), v_ref[...].astype(BF16),
                               preferred_element_type=F32)
        pvs.append(pv)
    if fox:
        carry_ref[...] = carry
    update(scs, pvs, acc_ref)

    @pl.when(step == pl.num_programs(1) - 1)
    def _():
        qf = fold(qbd.astype(F32)) if fox else qbd.astype(F32)
        tok = _div_pow2(lax.broadcasted_iota(jnp.int32, (rows, 1), 0), ngroups)
        for t in range(t_new):
            sc = jnp.sum(qf * kn_ref[t:t + 1, :], axis=1, keepdims=True)
            if fox:
                cn = x_ref[...]
                sc = sc + (cn - jnp.concatenate([cn[t * ngroups:(t + 1) * ngroups, :]] * t_new, axis=0))
            sc = jnp.where(tok >= t, sc, -jnp.inf)
            if fox:
                update([sc], [lambda p, t=t: p * vn_ref[t:t + 1, :]], acc_ref)
            else:
                update([sc], [lambda p, t=t: p * vn_ref[t:t + 1, :]], last_ref, others=(acc_ref,))
        if fox:
            weighted = acc_ref[...] * (1.0 / l_ref[...])
            for t in range(t_new):
                blk = jnp.where(own, weighted[t * ngroups:(t + 1) * ngroups, :], 0.0)
                o_ref[t:t + 1, :] = jnp.sum(blk, axis=0, keepdims=True)
        else:
            sub = jnp.bitwise_and(lax.broadcasted_iota(jnp.int32, (rows, 1), 0), 1)
            coef = jnp.where(sub == 0, 1.0, -_diff_lambda(x_ref, lam_init)) / l_ref[...]
            own_lanes = (_div_pow2(lax.broadcasted_iota(jnp.int32, (ngroups, d), 1), 2 * GROUP)
                         == _div_pow2(lax.broadcasted_iota(jnp.int32, (ngroups, d), 0), 2))
            cached = acc_ref[...] * coef
            fresh = last_ref[...] * coef
            for t in range(t_new):
                r0 = t * ngroups
                blk = jnp.where(own_lanes, fresh[r0:r0 + ngroups, :], 0.0)
                pairs = [cached[r0 + 2 * h:r0 + 2 * h + 1, :] + cached[r0 + 2 * h + 1:r0 + 2 * h + 2, :]
                         for h in range(ngroups // 2)]
                o_ref[t:t + 1, :] = jnp.sum(blk, axis=0, keepdims=True) + jnp.concatenate(pairs, axis=1)


def _decode_attention(page_table, q, k_new, v_new, extra, k_cache, v_cache, lf_cache=None, *, fox, lam_init=0.0):
    nb, t_new, d = q.shape
    ngroups = d // GROUP
    rows = t_new * ngroups
    n_pages = page_table.shape[1]
    page = k_cache.shape[2]
    pages = math.gcd(DECODE_PAGES_PER_STEP, n_pages)
    steps = n_pages // pages

    def tok_spec():
        return pl.BlockSpec((None, t_new, d), lambda b, s, pt: (b, 0, 0))

    def page_spec(g, shape):
        zeros = (0,) * len(shape)
        return pl.BlockSpec((None,) + shape, lambda b, s, pt: (pt[b, n_pages - 1 - (s * pages + g)],) + zeros)

    if fox:
        x_spec = pl.BlockSpec((None, rows, 1), lambda b, s, pt: (b, 0, 0))
    else:
        x_spec = pl.BlockSpec(extra.shape, lambda b, s, pt: (0, 0))
    in_specs = [tok_spec(), tok_spec(), tok_spec(), x_spec]
    in_specs += [page_spec(g, k_cache.shape[1:]) for g in range(pages)]
    in_specs += [page_spec(g, v_cache.shape[1:]) for g in range(pages)]
    args = [q, k_new, v_new, extra] + [k_cache] * pages + [v_cache] * pages
    scratch = [pltpu.VMEM(((2 if fox else 1) * rows, d), BF16), pltpu.VMEM((rows, 1), F32),
               pltpu.VMEM((rows, 1), F32)]
    if fox:
        j = lax.broadcasted_iota(jnp.int32, (page, MXU_COLS), 0)
        i = lax.broadcasted_iota(jnp.int32, (page, MXU_COLS), 1)
        u = jnp.where((j > i) | (i >= page), 1.0, 0.0).astype(BF16)
        in_specs += [page_spec(g, lf_cache.shape[1:]) for g in range(pages)]
        in_specs += [pl.BlockSpec(u.shape, lambda b, s, pt: (0, 0))]
        args += [lf_cache] * pages + [u]
        scratch += [pltpu.VMEM((rows, d), F32), pltpu.VMEM((ngroups, 1), F32)]
    else:
        width = v_cache.shape[1]
        heads = width // page
        spread = (_div_pow2(lax.broadcasted_iota(jnp.int32, (page, width), 1), heads)
                  == lax.broadcasted_iota(jnp.int32, (page, width), 0)).astype(BF16)
        in_specs += [pl.BlockSpec(spread.shape, lambda b, s, pt: (0, 0))]
        args += [spread]
        scratch += [pltpu.VMEM((rows, v_cache.shape[2]), F32), pltpu.VMEM((rows, d), F32)]
    return pl.pallas_call(
        functools.partial(_decode_attn_body, fox=fox, pages=pages, t_new=t_new, ngroups=ngroups, lam_init=lam_init),
        grid_spec=pltpu.PrefetchScalarGridSpec(
            num_scalar_prefetch=1, grid=(nb, steps), in_specs=in_specs, out_specs=tok_spec(),
            scratch_shapes=scratch),
        out_shape=jax.ShapeDtypeStruct((nb, t_new, d), F32),
        compiler_params=_params("parallel", "arbitrary"),
        name="fox_decode_attention" if fox else "diff_decode_attention",
    )(page_table, *args)


def _out_proj_body(*refs, diff, out_scale):
    if diff:
        h_ref, o_ref, w_ref, g_ref, seg_ref, exp_ref, y_ref = refs
        o = o_ref[...].astype(F32)
        o = o * _group_scale(o, seg_ref, exp_ref, 2 * GROUP) * g_ref[...] * out_scale
    else:
        h_ref, o_ref, w_ref, y_ref = refs
        o = o_ref[...]
    pieces = _split2(o) if w_ref.shape[0] == 2 else (o.astype(BF16),)
    y_ref[...] = h_ref[...] + _dot_pieces(pieces, w_ref)


def _out_project(h, o, w, *norm, diff, out_scale=1.0):
    n, d = h.shape
    tm = min(PROJ_ROWS, n)
    assert n % tm == 0
    return pl.pallas_call(
        functools.partial(_out_proj_body, diff=diff, out_scale=out_scale),
        grid=(n // tm,),
        in_specs=[_rows(tm, d), _rows(tm, d), _full(w.shape)] + [_full(a.shape) for a in norm],
        out_specs=_rows(tm, d),
        out_shape=jax.ShapeDtypeStruct((n, d), F32),
        compiler_params=_params("parallel"),
        name="attn_out_project",
    )(h, o, w, *norm)


def _route(logits, n_experts, n_groups):
    per_group = n_experts // n_groups
    lane = lax.broadcasted_iota(jnp.int32, logits.shape, 1)
    neg = -jnp.inf
    big = jnp.int32(LANES)

    def lane_max(x):
        return jnp.max(x, axis=1, keepdims=True)

    def first_lane(mask):
        return jnp.min(jnp.where(mask, lane, big), axis=1, keepdims=True)

    is_group = (lane >= n_experts) & (lane < n_experts + n_groups)
    g_logit = jnp.where(is_group, logits, neg)
    g_max = lane_max(g_logit)
    g_sel = first_lane(g_logit == g_max) - n_experts
    g_weight = 1.0 / jnp.sum(jnp.exp(g_logit - g_max), axis=1, keepdims=True)

    chosen = (lane < n_experts) & (_div_pow2(lane, per_group) == g_sel)
    e_logit = jnp.where(chosen, logits, neg)
    e_exp = jnp.exp(e_logit - lane_max(e_logit))
    e_prob = e_exp / jnp.sum(e_exp, axis=1, keepdims=True)
    w1 = lane_max(e_prob)
    i1 = first_lane(chosen & (e_prob == w1))
    rest = chosen & (lane != i1)
    e_rest = jnp.where(rest, e_prob, -1.0)
    w2 = lane_max(e_rest)
    i2 = first_lane(rest & (e_rest == w2))
    norm = g_weight / (w1 + w2)
    return jnp.where(lane == i1, w1 * norm, 0.0) + jnp.where(lane == i2, w2 * norm, 0.0)


def _moe_body(h_ref, fn_ref, wr_ref, br_ref, wg_ref, wu_ref, wd_ref, y_ref, xn_ref, gate_ref, *,
              n_experts, n_groups):
    e = pl.program_id(1)

    @pl.when(e == 0)
    def _():
        h = h_ref[...]
        xn = _rms_rows(h, fn_ref[...])
        xb = xn.astype(BF16)
        xn_ref[...] = xb
        x_lo = (xn - xb.astype(F32)).astype(BF16)
        logits = (jnp.dot(xb, wr_ref[0], preferred_element_type=F32)
                  + jnp.dot(xb, wr_ref[1], preferred_element_type=F32)
                  + jnp.dot(x_lo, wr_ref[0], preferred_element_type=F32)) + br_ref[...]
        gate_ref[...] = _route(logits, n_experts, n_groups)
        y_ref[...] = h

    xb = xn_ref[...]
    gate = jnp.dot(xb, wg_ref[...], preferred_element_type=F32)
    up = jnp.dot(xb, wu_ref[...], preferred_element_type=F32)
    hid = (gate * jax.nn.sigmoid(gate) * up).astype(BF16)
    out = jnp.dot(hid, wd_ref[...], preferred_element_type=F32)
    lane = lax.broadcasted_iota(jnp.int32, gate_ref.shape, 1)
    w = jnp.sum(jnp.where(lane == e, gate_ref[...], 0.0), axis=1, keepdims=True)
    y_ref[...] += out * w


def _moe(h, fn, wr, br, wg, wu, wd, *, n_groups):
    n, d = h.shape
    n_experts, _, ff = wg.shape
    tm = min(MOE_ROWS, n)
    assert n % tm == 0
    return pl.pallas_call(
        functools.partial(_moe_body, n_experts=n_experts, n_groups=n_groups),
        grid=(n // tm, n_experts),
        in_specs=[pl.BlockSpec((tm, d), lambda i, e: (i, 0)),
                  pl.BlockSpec(fn.shape, lambda i, e: (0, 0)),
                  pl.BlockSpec(wr.shape, lambda i, e: (0, 0, 0)),
                  pl.BlockSpec(br.shape, lambda i, e: (0, 0)),
                  pl.BlockSpec((None, d, ff), lambda i, e: (e, 0, 0)),
                  pl.BlockSpec((None, d, ff), lambda i, e: (e, 0, 0)),
                  pl.BlockSpec((None, ff, d), lambda i, e: (e, 0, 0))],
        out_specs=pl.BlockSpec((tm, d), lambda i, e: (i, 0)),
        out_shape=jax.ShapeDtypeStruct((n, d), F32),
        scratch_shapes=[pltpu.VMEM((tm, d), BF16), pltpu.VMEM((tm, LANES), F32)],
        compiler_params=_params("parallel", "arbitrary"),
        name="hier_moe",
    )(h, fn, wr, br, wg, wu, wd)


def _ple_body(h_ref, p_ref, g_ref, wg_ref, wp_ref, y_ref):
    h = h_ref[...]
    xb = _rms_rows(h, g_ref[...]).astype(BF16)
    gate = jax.nn.sigmoid(jnp.dot(xb, wg_ref[...], preferred_element_type=F32))
    proj = jnp.dot(p_ref[...].astype(BF16), wp_ref[...], preferred_element_type=F32)
    y_ref[...] = h + gate * proj


def _ple(h, p, g, wg, wp):
    n, d = h.shape
    tm = min(PROJ_ROWS, n)
    assert n % tm == 0
    return pl.pallas_call(
        _ple_body,
        grid=(n // tm,),
        in_specs=[_rows(tm, d), _rows(tm, p.shape[1]), _full(g.shape), _full(wg.shape), _full(wp.shape)],
        out_specs=_rows(tm, d),
        out_shape=jax.ShapeDtypeStruct((n, d), F32),
        compiler_params=_params("parallel"),
        name="per_layer_embedding",
    )(h, p, g, wg, wp)


def _segment_tables(d, group):
    lane_group = jnp.arange(d)[:, None] // group
    seg = (lane_group == jnp.arange(LANES)[None, :]).astype(BF16)
    return seg, jnp.concatenate([seg.T, seg.T], axis=0)


def _cumsum_table(tm, period):
    r = jnp.arange(tm)[:, None]
    s = jnp.arange(tm)[None, :]
    return ((s <= r) & (r // period == s // period)).astype(BF16)


def _row(vec, reps=1):
    return jnp.tile(vec.astype(F32), reps)[None, :]


def _hi_lo(w):
    w = w.astype(F32)
    hi = w.astype(BF16)
    return jnp.stack([hi, (w - hi.astype(F32)).astype(BF16)])


def _pad_lanes(x):
    return jnp.pad(x, [(0, 0)] * (x.ndim - 1) + [(0, LANES - x.shape[-1])])


def kernel(x_prompt, x_sample, p_prompt, p_sample, cache_fox_k, cache_fox_v, cache_fox_logf, cache_diff_k, cache_diff_v, page_table, attn_norm, fox_w_in, fox_b_f, fox_q_norm, fox_k_norm, fox_w_out, diff_w_in, diff_q_norm, diff_k_norm, diff_lambda, diff_subln, diff_w_out, ffn_norm, moe_w_group, moe_b_group, moe_w_expert, moe_b_expert, moe_w_gate, moe_w_up, moe_w_down, ple_norm, ple_w_gate, ple_w_proj):
    b, s, d = x_prompt.shape
    nb, t_new, _ = x_sample.shape
    depth = attn_norm.shape[0]
    n_pool, page = cache_fox_k.shape[1:3]
    past_len = page_table.shape[1] * page
    fox_heads = fox_b_f.shape[-1]
    diff_heads = d // (2 * GROUP)
    n_groups = moe_w_group.shape[-1]
    n_experts = moe_w_expert.shape[-1]
    assert fox_q_norm.shape[-1] == GROUP and diff_q_norm.shape[-1] == GROUP and fox_heads * GROUP == d
    assert 2 * fox_heads <= LANES and n_experts + n_groups <= LANES

    seg64, exp64 = _segment_tables(d, GROUP)
    seg128, exp128 = _segment_tables(d, 2 * GROUP)
    tri_p = _cumsum_table(min(PROJ_ROWS, b * s), s)
    tri_s = _cumsum_table(min(PROJ_ROWS, nb * t_new), t_new)
    rot = GROUP // ROPE_FRACTION
    inv_freq = jnp.power(jnp.float32(ROPE_THETA), -jnp.arange(rot // 2, dtype=jnp.float32) * (2.0 / rot))
    lane_d = jnp.arange(LANES) % GROUP
    invf = jnp.where(lane_d < rot, inv_freq[lane_d % (rot // 2)], 0.0)[None, :]

    fk_p, fv_p, ff_p, dk_p, dv_p = [], [], [], [], []
    fk_s, fv_s, ff_s, dk_s, dv_s = [], [], [], [], []
    hp = x_prompt.reshape(b * s, d)
    hs = x_sample.reshape(nb * t_new, d)
    for i in range(depth):
        j = i // N_MIXERS
        an = _row(attn_norm[i])
        if i % N_MIXERS == 0:
            w_qkv = _hi_lo(fox_w_in[j][:, :3 * d])
            wf = _pad_lanes(jnp.concatenate(list(_hi_lo(fox_w_in[j][:, 3 * d:])), axis=1))
            bf = _pad_lanes(_row(fox_b_f[j]))
            qg, kg = _row(fox_q_norm[j], fox_heads), _row(fox_k_norm[j], fox_heads)
            w_out = _hi_lo(fox_w_out[j])
            qp, kp, vp, lfp, cp, kbp, vbp, ctp = _fox_project(
                hp, an, w_qkv[:1], wf, bf, qg, kg, seg64, exp64, tri_p, period=s, prompt=True)
            qs, ks, vs, lfs, cs = _fox_project(
                hs, an, w_qkv, wf, bf, qg, kg, seg64, exp64, tri_s, period=t_new, prompt=False)
            op = _prompt_attention(qp.reshape(b, s, d), kbp.reshape(b, s, d), vbp.reshape(b, s, d),
                                   (cp.reshape(b, s, fox_heads), ctp), fox=True)
            os_ = _decode_attention(
                page_table, qs.reshape(nb, t_new, d), ks.reshape(nb, t_new, d), vs.reshape(nb, t_new, d),
                cs.reshape(nb, t_new * fox_heads, 1),
                jnp.transpose(cache_fox_k[j], (0, 2, 3, 1)).reshape(n_pool, d, page),
                jnp.transpose(cache_fox_v[j], (0, 2, 3, 1)).reshape(n_pool, d, page),
                jnp.transpose(cache_fox_logf[j], (0, 2, 1)), fox=True)
            hp = _out_project(hp, op.reshape(b * s, d), w_out[:1], diff=False)
            hs = _out_project(hs, os_.reshape(nb * t_new, d), w_out, diff=False)
            fk_p.append(kp.reshape(b, s, fox_heads, GROUP))
            fv_p.append(vp.reshape(b, s, fox_heads, GROUP))
            ff_p.append(lfp.reshape(b, s, fox_heads))
            fk_s.append(ks.reshape(nb, t_new, fox_heads, GROUP))
            fv_s.append(vs.reshape(nb, t_new, fox_heads, GROUP))
            ff_s.append(lfs.reshape(nb, t_new, fox_heads))
        else:
            lam_init = 0.8 - 0.6 * math.exp(-0.3 * i)
            w_qkv = diff_w_in[j].astype(BF16)
            qg, kg = _row(diff_q_norm[j], 2 * diff_heads), _row(diff_k_norm[j], 2 * diff_heads)
            w_out = diff_w_out[j].astype(BF16)[None]
            subln = _row(diff_subln[j], diff_heads)
            lam = diff_lambda[j].astype(F32)
            qp, kp, vp, kbp, vbp = _diff_project(hp, an, w_qkv, qg, kg, seg64, exp64, invf,
                                                 period=s, offset=0, prompt=True)
            qs, ks, vs = _diff_project(hs, an, w_qkv, qg, kg, seg64, exp64, invf,
                                       period=t_new, offset=past_len, prompt=False)
            op = _prompt_attention(qp.reshape(b, s, d), kbp.reshape(b, s, d), vbp.reshape(b, s, d),
                                   (lam,), fox=False, lam_init=lam_init)
            os_ = _decode_attention(
                page_table, qs.reshape(nb, t_new, d), ks.reshape(nb, t_new, d), vs.reshape(nb, t_new, d),
                lam, jnp.transpose(cache_diff_k[j], (0, 2, 3, 4, 1)).reshape(n_pool, d, page),
                cache_diff_v[j].reshape(n_pool, page * diff_heads, 2 * GROUP), fox=False, lam_init=lam_init)
            hp = _out_project(hp, op.reshape(b * s, d), w_out, subln, seg128, exp128,
                              diff=True, out_scale=1.0 - lam_init)
            hs = _out_project(hs, os_.reshape(nb * t_new, d), w_out, subln, seg128, exp128,
                              diff=True, out_scale=1.0 - lam_init)
            dk_p.append(kp.reshape(b, s, diff_heads, 2, GROUP))
            dv_p.append(vp.reshape(b, s, diff_heads, 2 * GROUP))
            dk_s.append(ks.reshape(nb, t_new, diff_heads, 2, GROUP))
            dv_s.append(vs.reshape(nb, t_new, diff_heads, 2 * GROUP))

        wr = _hi_lo(_pad_lanes(jnp.concatenate([moe_w_expert[i], moe_w_group[i]], axis=1)))
        br = _pad_lanes(_row(jnp.concatenate([moe_b_expert[i], moe_b_group[i]])))
        moe_args = (_row(ffn_norm[i]), wr, br, moe_w_gate[i].astype(BF16), moe_w_up[i].astype(BF16),
                    moe_w_down[i].astype(BF16))
        hp = _moe(hp, *moe_args, n_groups=n_groups)
        hs = _moe(hs, *moe_args, n_groups=n_groups)

        ple_args = (_row(ple_norm[i]), ple_w_gate[i].astype(BF16), ple_w_proj[i].astype(BF16))
        hp = _ple(hp, p_prompt[i].reshape(b * s, -1), *ple_args)
        hs = _ple(hs, p_sample[i].reshape(nb * t_new, -1), *ple_args)

    return (hp.reshape(b, s, d), hs.reshape(nb, t_new, d),
            jnp.stack(fk_p), jnp.stack(fv_p), jnp.stack(ff_p), jnp.stack(dk_p), jnp.stack(dv_p),
            jnp.stack(fk_s), jnp.stack(fv_s), jnp.stack(ff_s), jnp.stack(dk_s), jnp.stack(dv_s))
```

```python
import functools
import math

import jax
import jax.numpy as jnp
from jax import lax
from jax.experimental import pallas as pl
from jax.experimental.pallas import tpu as pltpu

F32 = jnp.float32
BF16 = jnp.bfloat16

LANES = 128
MXU_COLS = 256
GROUP = 64
VMEM_LIMIT_BYTES = 56 * 1024 * 1024
RMS_EPS = 1e-6
ROPE_THETA = 500000.0
ROPE_FRACTION = 4
N_MIXERS = 2
TOP_K = 2

PROJ_ROWS = 512
ATTN_ROWS = 512
MOE_ROWS = 1024
DECODE_PAGES_PER_STEP = 8


def _params(*sem):
    return pltpu.CompilerParams(dimension_semantics=sem, vmem_limit_bytes=VMEM_LIMIT_BYTES)


def _full(shape):
    zeros = (0,) * len(shape)
    return pl.BlockSpec(shape, lambda *_: zeros)


def _rows(tm, width):
    return pl.BlockSpec((tm, width), lambda i: (i, 0))


def _rms_rows(x, gain):
    return x * lax.rsqrt(jnp.mean(x * x, axis=-1, keepdims=True) + RMS_EPS) * gain


def _div_pow2(x, n):
    assert n & (n - 1) == 0
    return jnp.right_shift(x, n.bit_length() - 1)


def _split3(x):
    p1 = x.astype(BF16)
    r1 = x - p1.astype(F32)
    p2 = r1.astype(BF16)
    p3 = (r1 - p2.astype(F32)).astype(BF16)
    return p1, p2, p3


def _split2(x):
    hi = x.astype(BF16)
    return hi, (x - hi.astype(F32)).astype(BF16)


def _dot_pieces(xs, w_ref, cols=slice(None)):
    out = jnp.dot(xs[0], w_ref[0, :, cols], preferred_element_type=F32)
    if w_ref.shape[0] == 2:
        out = (out + jnp.dot(xs[1], w_ref[0, :, cols], preferred_element_type=F32)
               + jnp.dot(xs[0], w_ref[1, :, cols], preferred_element_type=F32))
    return out


def _group_scale(x, seg_ref, exp_ref, group):
    sq_hi, sq_lo = _split2(x * x)
    ss = (jnp.dot(sq_hi, seg_ref[...], preferred_element_type=F32)
          + jnp.dot(sq_lo, seg_ref[...], preferred_element_type=F32))
    hi, lo = _split2(lax.rsqrt(ss * (1.0 / group) + RMS_EPS))
    return jnp.dot(jnp.concatenate([hi, lo], axis=1), exp_ref[...], preferred_element_type=F32)


def _rope_tables(tile, tm, period, offset, invf):
    half = GROUP // ROPE_FRACTION // 2
    row = lax.broadcasted_iota(jnp.int32, (tm, LANES), 0) + tile * tm
    pos = (jnp.bitwise_and(row, period - 1) + offset).astype(F32)
    ang = pos * invf
    cs = jnp.cos(ang)
    sn = jnp.sin(ang)
    d = jnp.bitwise_and(lax.broadcasted_iota(jnp.int32, (tm, LANES), 1), GROUP - 1)
    return cs, jnp.where(d < half, -sn, 0.0), jnp.where(d >= half, sn, 0.0)


def _apply_rope(x, tables):
    cs, s_lo, s_hi = tables
    half = GROUP // ROPE_FRACTION // 2
    out = []
    for j in range(x.shape[1] // LANES):
        xs = x[:, j * LANES:(j + 1) * LANES]
        out.append(xs * cs + pltpu.roll(xs, LANES - half, 1) * s_lo + pltpu.roll(xs, half, 1) * s_hi)
    return jnp.concatenate(out, axis=1)


def _fox_proj_body(h_ref, an_ref, w_ref, wf_ref, bf_ref, qg_ref, kg_ref, seg_ref, exp_ref, tri_ref,
                   *rest, tm, d, nheads, period, prompt):
    if prompt:
        q_ref, kf_ref, vf_ref, lf_ref, c_ref, kb_ref, vb_ref, ct_ref, carry_ref = rest
    else:
        q_ref, kf_ref, vf_ref, lf_ref, c_ref, carry_ref = rest
    tile = pl.program_id(0)
    xn = _rms_rows(h_ref[...], an_ref[...])
    xb, x_lo = xs = _split2(xn)

    q = _dot_pieces(xs, w_ref, slice(0, d))
    q = q * _group_scale(q, seg_ref, exp_ref, GROUP) * qg_ref[...]
    q_ref[...] = (q * GROUP ** -0.5).astype(q_ref.dtype)
    k = _dot_pieces(xs, w_ref, slice(d, 2 * d))
    k = k * _group_scale(k, seg_ref, exp_ref, GROUP) * kg_ref[...]
    kf_ref[...] = k
    v = _dot_pieces(xs, w_ref, slice(2 * d, 3 * d))
    vf_ref[...] = v
    if prompt:
        kb_ref[...] = k.astype(BF16)
        vb_ref[...] = v.astype(BF16)

    z_hi = jnp.dot(xb, wf_ref[...], preferred_element_type=F32)
    z_lo = jnp.dot(x_lo, wf_ref[...], preferred_element_type=F32)
    z = z_hi + pltpu.roll(z_hi, LANES - nheads, 1) + z_lo + bf_ref[...]
    lf = jnp.minimum(z, 0.0) - jnp.log1p(jnp.exp(-jnp.abs(z)))
    lane = lax.broadcasted_iota(jnp.int32, (tm, LANES), 1)
    lf = jnp.where(lane < nheads, lf, 0.0)
    lf_ref[...] = lf[:, 0:nheads]

    p1, p2, p3 = _split3(lf)
    tri = tri_ref[...]
    c = (jnp.dot(tri, p1, preferred_element_type=F32) + jnp.dot(tri, p2, preferred_element_type=F32)
         + jnp.dot(tri, p3, preferred_element_type=F32))
    if period > tm:
        @pl.when(jnp.bitwise_and(tile * tm, period - 1) == 0)
        def _():
            carry_ref[...] = jnp.zeros_like(carry_ref)
        c = c + carry_ref[...]
        carry_ref[...] = c[tm - 1:tm, :]
    c_ref[...] = c[:, 0:nheads]
    if prompt:
        ct_ref[...] = c.T[0:nheads, :]


def _fox_project(h, an, w_qkv, wf, bf, qg, kg, seg, exp, tri, *, period, prompt):
    n, d = h.shape
    nheads = d // GROUP
    tm = min(PROJ_ROWS, n)
    assert n % tm == 0 and period & (period - 1) == 0 and (period % tm == 0 or tm % period == 0)
    row_f32 = jax.ShapeDtypeStruct((n, d), F32)
    row_bf16 = jax.ShapeDtypeStruct((n, d), BF16)
    head_f32 = jax.ShapeDtypeStruct((n, nheads), F32)
    out_shape = [row_bf16 if prompt else row_f32, row_f32, row_f32, head_f32, head_f32]
    out_specs = [_rows(tm, d), _rows(tm, d), _rows(tm, d), _rows(tm, nheads), _rows(tm, nheads)]
    if prompt:
        out_shape += [row_bf16, row_bf16, jax.ShapeDtypeStruct((nheads, n), F32)]
        out_specs += [_rows(tm, d), _rows(tm, d), pl.BlockSpec((nheads, tm), lambda i: (0, i))]
    return pl.pallas_call(
        functools.partial(_fox_proj_body, tm=tm, d=d, nheads=nheads, period=period, prompt=prompt),
        grid=(n // tm,),
        in_specs=[_rows(tm, d), _full(an.shape), _full(w_qkv.shape), _full(wf.shape), _full(bf.shape),
                  _full(qg.shape), _full(kg.shape), _full(seg.shape), _full(exp.shape), _full(tri.shape)],
        out_specs=out_specs,
        out_shape=out_shape,
        scratch_shapes=[pltpu.VMEM((1, LANES), F32)],
        compiler_params=_params("arbitrary"),
        name="fox_project",
    )(h, an, w_qkv, wf, bf, qg, kg, seg, exp, tri)


def _diff_proj_body(h_ref, an_ref, w_ref, qg_ref, kg_ref, seg_ref, exp_ref, invf_ref, *rest,
                    tm, d, period, offset, prompt):
    if prompt:
        q_ref, kf_ref, vf_ref, kb_ref, vb_ref = rest
    else:
        q_ref, kf_ref, vf_ref = rest
    xb = _rms_rows(h_ref[...], an_ref[...]).astype(BF16)
    tables = _rope_tables(pl.program_id(0), tm, period, offset, invf_ref[...])

    q = jnp.dot(xb, w_ref[:, 0:d], preferred_element_type=F32)
    q = _apply_rope(q * _group_scale(q, seg_ref, exp_ref, GROUP) * qg_ref[...], tables)
    q_ref[...] = (q * GROUP ** -0.5).astype(BF16)
    k = jnp.dot(xb, w_ref[:, d:2 * d], preferred_element_type=F32)
    k = _apply_rope(k * _group_scale(k, seg_ref, exp_ref, GROUP) * kg_ref[...], tables)
    kf_ref[...] = k
    v = jnp.dot(xb, w_ref[:, 2 * d:3 * d], preferred_element_type=F32)
    vf_ref[...] = v
    if prompt:
        kb_ref[...] = k.astype(BF16)
        vb_ref[...] = v.astype(BF16)


def _diff_project(h, an, w_qkv, qg, kg, seg, exp, invf, *, period, offset, prompt):
    n, d = h.shape
    tm = min(PROJ_ROWS, n)
    assert n % tm == 0 and period & (period - 1) == 0
    row_f32 = jax.ShapeDtypeStruct((n, d), F32)
    row_bf16 = jax.ShapeDtypeStruct((n, d), BF16)
    n_out = 5 if prompt else 3
    return pl.pallas_call(
        functools.partial(_diff_proj_body, tm=tm, d=d, period=period, offset=offset, prompt=prompt),
        grid=(n // tm,),
        in_specs=[_rows(tm, d), _full(an.shape), _full(w_qkv.shape), _full(qg.shape), _full(kg.shape),
                  _full(seg.shape), _full(exp.shape), _full(invf.shape)],
        out_specs=[_rows(tm, d)] * n_out,
        out_shape=[row_bf16, row_f32, row_f32, row_bf16, row_bf16][:n_out],
        compiler_params=_params("parallel"),
        name="diff_project",
    )(h, an, w_qkv, qg, kg, seg, exp, invf)


def _diff_lambda(lam_ref, lam_init):
    lp = lam_ref[...]
    a = jnp.sum(lp[0:1, :] * lp[1:2, :], axis=1, keepdims=True)
    b = jnp.sum(lp[2:3, :] * lp[3:4, :], axis=1, keepdims=True)
    return jnp.exp(a) - jnp.exp(b) + lam_init


def _prompt_attn_body(*refs, fox, tq, lam_init):
    if fox:
        q_ref, k_ref, v_ref, c_ref, ct_ref, o_ref = refs
    else:
        q_ref, k_ref, v_ref, lam_ref, o_ref = refs
    col = pl.program_id(1)
    qi = pl.program_id(2)
    q2 = q_ref[...].astype(F32)
    lo = lax.broadcasted_iota(jnp.int32, (tq, LANES), 1) < GROUP
    qs = (jnp.where(lo, q2, 0.0).astype(BF16), jnp.where(lo, 0.0, q2).astype(BF16))
    if fox:
        c_tile = c_ref[...]
        head = lax.broadcasted_iota(jnp.int32, c_tile.shape, 1)
        cq = tuple(jnp.sum(jnp.where(head == 2 * col + s, c_tile, 0.0), axis=1, keepdims=True) for s in (0, 1))
    causal = (lax.broadcasted_iota(jnp.int32, (tq, tq), 0) >= lax.broadcasted_iota(jnp.int32, (tq, tq), 1))

    def tile(kj, carry, diagonal):
        start = pl.multiple_of(kj * tq, tq)
        kt = k_ref[pl.ds(start, tq), :]
        vt = v_ref[pl.ds(start, tq), :]
        new = []
        for s in (0, 1):
            m, l, acc = carry[s]
            sc = lax.dot_general(qs[s], kt, (((1,), (1,)), ((), ())), preferred_element_type=F32)
            if fox:
                sc = sc + (cq[s] - ct_ref[pl.ds(2 * col + s, 1), pl.ds(start, tq)])
            if diagonal:
                sc = jnp.where(causal, sc, -jnp.inf)
            m_new = jnp.maximum(m, jnp.max(sc, axis=1, keepdims=True))
            alpha = jnp.exp(m - m_new)
            p = jnp.exp(sc - m_new)
            l_new = alpha * l + jnp.sum(p, axis=1, keepdims=True)
            acc_new = alpha * acc + jnp.dot(p.astype(BF16), vt, preferred_element_type=F32)
            new.append((m_new, l_new, acc_new))
        return tuple(new)

    init = tuple((jnp.full((tq, 1), -jnp.inf, F32), jnp.zeros((tq, 1), F32), jnp.zeros((tq, LANES), F32))
                 for _ in (0, 1))
    carry = lax.fori_loop(0, qi, lambda kj, c: tile(kj, c, False), init)
    (_, l0, a0), (_, l1, a1) = tile(qi, carry, True)
    if fox:
        o_ref[...] = jnp.where(lo, a0 / l0, a1 / l1).astype(o_ref.dtype)
    else:
        o_ref[...] = (a0 / l0 - _diff_lambda(lam_ref, lam_init) * (a1 / l1)).astype(o_ref.dtype)


def _prompt_attention(q, k, v, extra, *, fox, lam_init=0.0):
    b, s, d = q.shape
    tq = min(ATTN_ROWS, s)
    assert s % tq == 0 and d % LANES == 0
    q_spec = pl.BlockSpec((None, tq, LANES), lambda bi, ci, qi: (bi, qi, ci))
    kv_spec = pl.BlockSpec((None, s, LANES), lambda bi, ci, qi: (bi, 0, ci))
    if fox:
        c, ct = extra
        nheads = c.shape[-1]
        extra_specs = [pl.BlockSpec((None, tq, nheads), lambda bi, ci, qi: (bi, qi, 0)),
                       pl.BlockSpec((nheads, s), lambda bi, ci, qi: (0, bi))]
    else:
        extra_specs = [_full(extra[0].shape)]
    return pl.pallas_call(
        functools.partial(_prompt_attn_body, fox=fox, tq=tq, lam_init=lam_init),
        grid=(b, d // LANES, s // tq),
        in_specs=[q_spec, kv_spec, kv_spec] + extra_specs,
        out_specs=q_spec,
        out_shape=jax.ShapeDtypeStruct((b, s, d), BF16 if fox else F32),
        compiler_params=_params("parallel", "parallel", "arbitrary"),
        name="fox_prompt_attention" if fox else "diff_prompt_attention",
    )(q, k, v, *extra)


def _decode_attn_body(pt_ref, *refs, fox, pages, t_new, ngroups, lam_init):
    del pt_ref
    q_ref, kn_ref, vn_ref, x_ref = refs[:4]
    k_refs = refs[4:4 + pages]
    v_refs = refs[4 + pages:4 + 2 * pages]
    pos = 4 + 2 * pages
    if fox:
        lf_refs = refs[pos:pos + pages]
        u_ref = refs[pos + pages]
        pos += pages + 1
    else:
        spread_ref = refs[pos]
        pos += 1
    o_ref, qbd_ref, m_ref, l_ref, acc_ref, last_ref = refs[pos:pos + 6]
    step = pl.program_id(1)
    rows = t_new * ngroups
    d = q_ref.shape[-1]
    own = (_div_pow2(lax.broadcasted_iota(jnp.int32, (ngroups, d), 1), GROUP)
           == lax.broadcasted_iota(jnp.int32, (ngroups, d), 0))

    @pl.when(step == 0)
    def _():
        q = q_ref[...].astype(F32)
        for t in range(t_new):
            row = jnp.where(own, jnp.broadcast_to(q[t:t + 1, :], (ngroups, d)), 0.0)
            if fox:
                hi, lo = _split2(row)
                qbd_ref[t * ngroups:(t + 1) * ngroups, :] = hi
                qbd_ref[rows + t * ngroups:rows + (t + 1) * ngroups, :] = lo
            else:
                qbd_ref[t * ngroups:(t + 1) * ngroups, :] = row.astype(BF16)
        m_ref[...] = jnp.full_like(m_ref, -jnp.inf)
        l_ref[...] = jnp.zeros_like(l_ref)
        acc_ref[...] = jnp.zeros_like(acc_ref)
        last_ref[...] = jnp.zeros_like(last_ref)

    carry_ref = last_ref

    def update(scs, pvs, target, others=()):
        m = m_ref[...]
        m_new = m
        for sc in scs:
            m_new = jnp.maximum(m_new, jnp.max(sc, axis=1, keepdims=True))
        alpha = jnp.exp(m - m_new)
        ps = [jnp.exp(sc - m_new) for sc in scs]
        l_ref[...] = alpha * l_ref[...] + sum(jnp.sum(p, axis=1, keepdims=True) for p in ps)
        target[...] = alpha * target[...] + sum(pv(p) for pv, p in zip(pvs, ps))
        for ref in others:
            ref[...] = alpha * ref[...]
        m_ref[...] = m_new

    def fold(x):
        return x[0:rows] + x[rows:2 * rows]

    qbd = qbd_ref[...]
    if not fox:
        spread = spread_ref[...]
        heads = spread.shape[1] // spread.shape[0]
        own_head = (jnp.bitwise_and(lax.broadcasted_iota(jnp.int32, (rows, spread.shape[1]), 1), heads - 1)
                    == _div_pow2(jnp.bitwise_and(lax.broadcasted_iota(jnp.int32, (rows, spread.shape[1]), 0),
                                                 ngroups - 1), ngroups // heads))
    scs, pvs = [], []
    if fox:
        carry = carry_ref[...]
    for g in range(pages):
        if fox:
            k_hi, k_lo = _split2(k_refs[g][...])
            sc = (fold(jnp.dot(qbd, k_hi, preferred_element_type=F32))
                  + jnp.dot(qbd[0:rows], k_lo, preferred_element_type=F32))
            u = u_ref[...]
            suf = sum(jnp.dot(p, u, preferred_element_type=F32) for p in _split3(lf_refs[g][...]))
            page = u.shape[0]
            decay = suf[:, 0:page] + carry
            carry = carry + suf[:, page:page + 1]
            scs.append(sc + (jnp.concatenate([decay] * t_new, axis=0) + x_ref[...]))
            by_key = (((1,), (1,)), ((), ()))

            def pv(p, v_ref=v_refs[g]):
                v_hi, v_lo = _split2(v_ref[...])
                p_hi, p_lo = _split2(p)
                both = jnp.concatenate([p_hi, p_lo], axis=0)
                return (fold(lax.dot_general(both, v_hi, by_key, preferred_element_type=F32))
                        + lax.dot_general(p_hi, v_lo, by_key, preferred_element_type=F32))
        else:
            scs.append(jnp.dot(qbd, k_refs[g][...].astype(BF16), preferred_element_type=F32))

            def pv(p, v_ref=v_refs[g]):
                wide = jnp.dot(p.astype(BF16), spread, preferred_element_type=F32)
                return jnp.dot(jnp.where(own_head, wide, 0.0).astype(BF16), v_ref[...].astype(BF16),
                               preferred_element_type=F32)
        pvs.append(pv)
    if fox:
        carry_ref[...] = carry
    update(scs, pvs, acc_ref)

    @pl.when(step == pl.num_programs(1) - 1)
    def _():
        qf = fold(qbd.astype(F32)) if fox else qbd.astype(F32)
        tok = _div_pow2(lax.broadcasted_iota(jnp.int32, (rows, 1), 0), ngroups)
        for t in range(t_new):
            sc = jnp.sum(qf * kn_ref[t:t + 1, :], axis=1, keepdims=True)
            if fox:
                cn = x_ref[...]
                sc = sc + (cn - jnp.concatenate([cn[t * ngroups:(t + 1) * ngroups, :]] * t_new, axis=0))
            sc = jnp.where(tok >= t, sc, -jnp.inf)
            if fox:
                update([sc], [lambda p, t=t: p * vn_ref[t:t + 1, :]], acc_ref)
            else:
                update([sc], [lambda p, t=t: p * vn_ref[t:t + 1, :]], last_ref, others=(acc_ref,))
        if fox:
            weighted = acc_ref[...] * (1.0 / l_ref[...])
            for t in range(t_new):
                blk = jnp.where(own, weighted[t * ngroups:(t + 1) * ngroups, :], 0.0)
                o_ref[t:t + 1, :] = jnp.sum(blk, axis=0, keepdims=True)
        else:
            sub = jnp.bitwise_and(lax.broadcasted_iota(jnp.int32, (rows, 1), 0), 1)
            coef = jnp.where(sub == 0, 1.0, -_diff_lambda(x_ref, lam_init)) / l_ref[...]
            own_lanes = (_div_pow2(lax.broadcasted_iota(jnp.int32, (ngroups, d), 1), 2 * GROUP)
                         == _div_pow2(lax.broadcasted_iota(jnp.int32, (ngroups, d), 0), 2))
            cached = acc_ref[...] * coef
            fresh = last_ref[...] * coef
            for t in range(t_new):
                r0 = t * ngroups
                blk = jnp.where(own_lanes, fresh[r0:r0 + ngroups, :], 0.0)
                pairs = [cached[r0 + 2 * h:r0 + 2 * h + 1, :] + cached[r0 + 2 * h + 1:r0 + 2 * h + 2, :]
                         for h in range(ngroups // 2)]
                o_ref[t:t + 1, :] = jnp.sum(blk, axis=0, keepdims=True) + jnp.concatenate(pairs, axis=1)


def _decode_attention(page_table, q, k_new, v_new, extra, k_cache, v_cache, lf_cache=None, *, fox, lam_init=0.0):
    nb, t_new, d = q.shape
    ngroups = d // GROUP
    rows = t_new * ngroups
    n_pages = page_table.shape[1]
    page = k_cache.shape[2]
    pages = math.gcd(DECODE_PAGES_PER_STEP, n_pages)
    steps = n_pages // pages

    def tok_spec():
        return pl.BlockSpec((None, t_new, d), lambda b, s, pt: (b, 0, 0))

    def page_spec(g, shape):
        zeros = (0,) * len(shape)
        return pl.BlockSpec((None,) + shape, lambda b, s, pt: (pt[b, n_pages - 1 - (s * pages + g)],) + zeros)

    if fox:
        x_spec = pl.BlockSpec((None, rows, 1), lambda b, s, pt: (b, 0, 0))
    else:
        x_spec = pl.BlockSpec(extra.shape, lambda b, s, pt: (0, 0))
    in_specs = [tok_spec(), tok_spec(), tok_spec(), x_spec]
    in_specs += [page_spec(g, k_cache.shape[1:]) for g in range(pages)]
    in_specs += [page_spec(g, v_cache.shape[1:]) for g in range(pages)]
    args = [q, k_new, v_new, extra] + [k_cache] * pages + [v_cache] * pages
    scratch = [pltpu.VMEM(((2 if fox else 1) * rows, d), BF16), pltpu.VMEM((rows, 1), F32),
               pltpu.VMEM((rows, 1), F32)]
    if fox:
        j = lax.broadcasted_iota(jnp.int32, (page, MXU_COLS), 0)
        i = lax.broadcasted_iota(jnp.int32, (page, MXU_COLS), 1)
        u = jnp.where((j > i) | (i >= page), 1.0, 0.0).astype(BF16)
        in_specs += [page_spec(g, lf_cache.shape[1:]) for g in range(pages)]
        in_specs += [pl.BlockSpec(u.shape, lambda b, s, pt: (0, 0))]
        args += [lf_cache] * pages + [u]
        scratch += [pltpu.VMEM((rows, d), F32), pltpu.VMEM((ngroups, 1), F32)]
    else:
        width = v_cache.shape[1]
        heads = width // page
        spread = (_div_pow2(lax.broadcasted_iota(jnp.int32, (page, width), 1), heads)
                  == lax.broadcasted_iota(jnp.int32, (page, width), 0)).astype(BF16)
        in_specs += [pl.BlockSpec(spread.shape, lambda b, s, pt: (0, 0))]
        args += [spread]
        scratch += [pltpu.VMEM((rows, v_cache.shape[2]), F32), pltpu.VMEM((rows, d), F32)]
    return pl.pallas_call(
        functools.partial(_decode_attn_body, fox=fox, pages=pages, t_new=t_new, ngroups=ngroups, lam_init=lam_init),
        grid_spec=pltpu.PrefetchScalarGridSpec(
            num_scalar_prefetch=1, grid=(nb, steps), in_specs=in_specs, out_specs=tok_spec(),
            scratch_shapes=scratch),
        out_shape=jax.ShapeDtypeStruct((nb, t_new, d), F32),
        compiler_params=_params("parallel", "arbitrary"),
        name="fox_decode_attention" if fox else "diff_decode_attention",
    )(page_table, *args)


def _out_proj_body(*refs, diff, out_scale):
    if diff:
        h_ref, o_ref, w_ref, g_ref, seg_ref, exp_ref, y_ref = refs
        o = o_ref[...].astype(F32)
        o = o * _group_scale(o, seg_ref, exp_ref, 2 * GROUP) * g_ref[...] * out_scale
    else:
        h_ref, o_ref, w_ref, y_ref = refs
        o = o_ref[...]
    pieces = _split2(o) if w_ref.shape[0] == 2 else (o.astype(BF16),)
    y_ref[...] = h_ref[...] + _dot_pieces(pieces, w_ref)


def _out_project(h, o, w, *norm, diff, out_scale=1.0):
    n, d = h.shape
    tm = min(PROJ_ROWS, n)
    assert n % tm == 0
    return pl.pallas_call(
        functools.partial(_out_proj_body, diff=diff, out_scale=out_scale),
        grid=(n // tm,),
        in_specs=[_rows(tm, d), _rows(tm, d), _full(w.shape)] + [_full(a.shape) for a in norm],
        out_specs=_rows(tm, d),
        out_shape=jax.ShapeDtypeStruct((n, d), F32),
        compiler_params=_params("parallel"),
        name="attn_out_project",
    )(h, o, w, *norm)


def _route(logits, n_experts, n_groups):
    per_group = n_experts // n_groups
    lane = lax.broadcasted_iota(jnp.int32, logits.shape, 1)
    neg = -jnp.inf
    big = jnp.int32(LANES)

    def lane_max(x):
        return jnp.max(x, axis=1, keepdims=True)

    def first_lane(mask):
        return jnp.min(jnp.where(mask, lane, big), axis=1, keepdims=True)

    is_group = (lane >= n_experts) & (lane < n_experts + n_groups)
    g_logit = jnp.where(is_group, logits, neg)
    g_max = lane_max(g_logit)
    g_sel = first_lane(g_logit == g_max) - n_experts
    g_weight = 1.0 / jnp.sum(jnp.exp(g_logit - g_max), axis=1, keepdims=True)

    chosen = (lane < n_experts) & (_div_pow2(lane, per_group) == g_sel)
    e_logit = jnp.where(chosen, logits, neg)
    e_exp = jnp.exp(e_logit - lane_max(e_logit))
    e_prob = e_exp / jnp.sum(e_exp, axis=1, keepdims=True)
    w1 = lane_max(e_prob)
    i1 = first_lane(chosen & (e_prob == w1))
    rest = chosen & (lane != i1)
    e_rest = jnp.where(rest, e_prob, -1.0)
    w2 = lane_max(e_rest)
    i2 = first_lane(rest & (e_rest == w2))
    norm = g_weight / (w1 + w2)
    return jnp.where(lane == i1, w1 * norm, 0.0) + jnp.where(lane == i2, w2 * norm, 0.0)


def _moe_body(h_ref, fn_ref, wr_ref, br_ref, wg_ref, wu_ref, wd_ref, y_ref, xn_ref, gate_ref, *,
              n_experts, n_groups):
    e = pl.program_id(1)

    @pl.when(e == 0)
    def _():
        h = h_ref[...]
        xn = _rms_rows(h, fn_ref[...])
        xb = xn.astype(BF16)
        xn_ref[...] = xb
        x_lo = (xn - xb.astype(F32)).astype(BF16)
        logits = (jnp.dot(xb, wr_ref[0], preferred_element_type=F32)
                  + jnp.dot(xb, wr_ref[1], preferred_element_type=F32)
                  + jnp.dot(x_lo, wr_ref[0], preferred_element_type=F32)) + br_ref[...]
        gate_ref[...] = _route(logits, n_experts, n_groups)
        y_ref[...] = h

    xb = xn_ref[...]
    gate = jnp.dot(xb, wg_ref[...], preferred_element_type=F32)
    up = jnp.dot(xb, wu_ref[...], preferred_element_type=F32)
    hid = (gate * jax.nn.sigmoid(gate) * up).astype(BF16)
    out = jnp.dot(hid, wd_ref[...], preferred_element_type=F32)
    lane = lax.broadcasted_iota(jnp.int32, gate_ref.shape, 1)
    w = jnp.sum(jnp.where(lane == e, gate_ref[...], 0.0), axis=1, keepdims=True)
    y_ref[...] += out * w


def _moe(h, fn, wr, br, wg, wu, wd, *, n_groups):
    n, d = h.shape
    n_experts, _, ff = wg.shape
    tm = min(MOE_ROWS, n)
    assert n % tm == 0
    return pl.pallas_call(
        functools.partial(_moe_body, n_experts=n_experts, n_groups=n_groups),
        grid=(n // tm, n_experts),
        in_specs=[pl.BlockSpec((tm, d), lambda i, e: (i, 0)),
                  pl.BlockSpec(fn.shape, lambda i, e: (0, 0)),
                  pl.BlockSpec(wr.shape, lambda i, e: (0, 0, 0)),
                  pl.BlockSpec(br.shape, lambda i, e: (0, 0)),
                  pl.BlockSpec((None, d, ff), lambda i, e: (e, 0, 0)),
                  pl.BlockSpec((None, d, ff), lambda i, e: (e, 0, 0)),
                  pl.BlockSpec((None, ff, d), lambda i, e: (e, 0, 0))],
        out_specs=pl.BlockSpec((tm, d), lambda i, e: (i, 0)),
        out_shape=jax.ShapeDtypeStruct((n, d), F32),
        scratch_shapes=[pltpu.VMEM((tm, d), BF16), pltpu.VMEM((tm, LANES), F32)],
        compiler_params=_params("parallel", "arbitrary"),
        name="hier_moe",
    )(h, fn, wr, br, wg, wu, wd)


def _ple_body(h_ref, p_ref, g_ref, wg_ref, wp_ref, y_ref):
    h = h_ref[...]
    xb = _rms_rows(h, g_ref[...]).astype(BF16)
    gate = jax.nn.sigmoid(jnp.dot(xb, wg_ref[...], preferred_element_type=F32))
    proj = jnp.dot(p_ref[...].astype(BF16), wp_ref[...], preferred_element_type=F32)
    y_ref[...] = h + gate * proj


def _ple(h, p, g, wg, wp):
    n, d = h.shape
    tm = min(PROJ_ROWS, n)
    assert n % tm == 0
    return pl.pallas_call(
        _ple_body,
        grid=(n // tm,),
        in_specs=[_rows(tm, d), _rows(tm, p.shape[1]), _full(g.shape), _full(wg.shape), _full(wp.shape)],
        out_specs=_rows(tm, d),
        out_shape=jax.ShapeDtypeStruct((n, d), F32),
        compiler_params=_params("parallel"),
        name="per_layer_embedding",
    )(h, p, g, wg, wp)


def _segment_tables(d, group):
    lane_group = jnp.arange(d)[:, None] // group
    seg = (lane_group == jnp.arange(LANES)[None, :]).astype(BF16)
    return seg, jnp.concatenate([seg.T, seg.T], axis=0)


def _cumsum_table(tm, period):
    r = jnp.arange(tm)[:, None]
    s = jnp.arange(tm)[None, :]
    return ((s <= r) & (r // period == s // period)).astype(BF16)


def _row(vec, reps=1):
    return jnp.tile(vec.astype(F32), reps)[None, :]


def _hi_lo(w):
    w = w.astype(F32)
    bits = lax.bitcast_convert_type(w, jnp.uint32)
    bits = (bits + jnp.uint32(0x7FFF) + ((bits >> 16) & jnp.uint32(1))) & jnp.uint32(0xFFFF0000)
    top = lax.bitcast_convert_type(bits, F32)
    return jnp.stack([top.astype(BF16), (w - top).astype(BF16)])


def _pad_lanes(x):
    return jnp.pad(x, [(0, 0)] * (x.ndim - 1) + [(0, LANES - x.shape[-1])])


def kernel(x_prompt, x_sample, p_prompt, p_sample, cache_fox_k, cache_fox_v, cache_fox_logf, cache_diff_k, cache_diff_v, page_table, attn_norm, fox_w_in, fox_b_f, fox_q_norm, fox_k_norm, fox_w_out, diff_w_in, diff_q_norm, diff_k_norm, diff_lambda, diff_subln, diff_w_out, ffn_norm, moe_w_group, moe_b_group, moe_w_expert, moe_b_expert, moe_w_gate, moe_w_up, moe_w_down, ple_norm, ple_w_gate, ple_w_proj):
    b, s, d = x_prompt.shape
    nb, t_new, _ = x_sample.shape
    depth = attn_norm.shape[0]
    n_pool, page = cache_fox_k.shape[1:3]
    past_len = page_table.shape[1] * page
    fox_heads = fox_b_f.shape[-1]
    diff_heads = d // (2 * GROUP)
    n_groups = moe_w_group.shape[-1]
    n_experts = moe_w_expert.shape[-1]
    assert fox_q_norm.shape[-1] == GROUP and diff_q_norm.shape[-1] == GROUP and fox_heads * GROUP == d
    assert 2 * fox_heads <= LANES and n_experts + n_groups <= LANES

    seg64, exp64 = _segment_tables(d, GROUP)
    seg128, exp128 = _segment_tables(d, 2 * GROUP)
    tri_p = _cumsum_table(min(PROJ_ROWS, b * s), s)
    tri_s = _cumsum_table(min(PROJ_ROWS, nb * t_new), t_new)
    rot = GROUP // ROPE_FRACTION
    inv_freq = jnp.power(jnp.float32(ROPE_THETA), -jnp.arange(rot // 2, dtype=jnp.float32) * (2.0 / rot))
    lane_d = jnp.arange(LANES) % GROUP
    invf = jnp.where(lane_d < rot, inv_freq[lane_d % (rot // 2)], 0.0)[None, :]

    fk_p, fv_p, ff_p, dk_p, dv_p = [], [], [], [], []
    fk_s, fv_s, ff_s, dk_s, dv_s = [], [], [], [], []
    hp = x_prompt.reshape(b * s, d)
    hs = x_sample.reshape(nb * t_new, d)
    for i in range(depth):
        j = i // N_MIXERS
        an = _row(attn_norm[i])
        if i % N_MIXERS == 0:
            w_qkv = _hi_lo(fox_w_in[j][:, :3 * d])
            wf = _pad_lanes(jnp.concatenate(list(_hi_lo(fox_w_in[j][:, 3 * d:])), axis=1))
            bf = _pad_lanes(_row(fox_b_f[j]))
            qg, kg = _row(fox_q_norm[j], fox_heads), _row(fox_k_norm[j], fox_heads)
            w_out = _hi_lo(fox_w_out[j])
            qp, kp, vp, lfp, cp, kbp, vbp, ctp = _fox_project(
                hp, an, w_qkv[:1], wf, bf, qg, kg, seg64, exp64, tri_p, period=s, prompt=True)
            qs, ks, vs, lfs, cs = _fox_project(
                hs, an, w_qkv, wf, bf, qg, kg, seg64, exp64, tri_s, period=t_new, prompt=False)
            op = _prompt_attention(qp.reshape(b, s, d), kbp.reshape(b, s, d), vbp.reshape(b, s, d),
                                   (cp.reshape(b, s, fox_heads), ctp), fox=True)
            os_ = _decode_attention(
                page_table, qs.reshape(nb, t_new, d), ks.reshape(nb, t_new, d), vs.reshape(nb, t_new, d),
                cs.reshape(nb, t_new * fox_heads, 1),
                jnp.transpose(cache_fox_k[j], (0, 2, 3, 1)).reshape(n_pool, d, page),
                jnp.transpose(cache_fox_v[j], (0, 2, 3, 1)).reshape(n_pool, d, page),
                jnp.transpose(cache_fox_logf[j], (0, 2, 1)), fox=True)
            hp = _out_project(hp, op.reshape(b * s, d), w_out[:1], diff=False)
            hs = _out_project(hs, os_.reshape(nb * t_new, d), w_out, diff=False)
            fk_p.append(kp.reshape(b, s, fox_heads, GROUP))
            fv_p.append(vp.reshape(b, s, fox_heads, GROUP))
            ff_p.append(lfp.reshape(b, s, fox_heads))
            fk_s.append(ks.reshape(nb, t_new, fox_heads, GROUP))
            fv_s.append(vs.reshape(nb, t_new, fox_heads, GROUP))
            ff_s.append(lfs.reshape(nb, t_new, fox_heads))
        else:
            lam_init = 0.8 - 0.6 * math.exp(-0.3 * i)
            w_qkv = diff_w_in[j].astype(BF16)
            qg, kg = _row(diff_q_norm[j], 2 * diff_heads), _row(diff_k_norm[j], 2 * diff_heads)
            w_out = diff_w_out[j].astype(BF16)[None]
            subln = _row(diff_subln[j], diff_heads)
            lam = diff_lambda[j].astype(F32)
            qp, kp, vp, kbp, vbp = _diff_project(hp, an, w_qkv, qg, kg, seg64, exp64, invf,
                                                 period=s, offset=0, prompt=True)
            qs, ks, vs = _diff_project(hs, an, w_qkv, qg, kg, seg64, exp64, invf,
                                       period=t_new, offset=past_len, prompt=False)
            op = _prompt_attention(qp.reshape(b, s, d), kbp.reshape(b, s, d), vbp.reshape(b, s, d),
                                   (lam,), fox=False, lam_init=lam_init)
            os_ = _decode_attention(
                page_table, qs.reshape(nb, t_new, d), ks.reshape(nb, t_new, d), vs.reshape(nb, t_new, d),
                lam, jnp.transpose(cache_diff_k[j], (0, 2, 3, 4, 1)).reshape(n_pool, d, page),
                cache_diff_v[j].reshape(n_pool, page * diff_heads, 2 * GROUP), fox=False, lam_init=lam_init)
            hp = _out_project(hp, op.reshape(b * s, d), w_out, subln, seg128, exp128,
                              diff=True, out_scale=1.0 - lam_init)
            hs = _out_project(hs, os_.reshape(nb * t_new, d), w_out, subln, seg128, exp128,
                              diff=True, out_scale=1.0 - lam_init)
            dk_p.append(kp.reshape(b, s, diff_heads, 2, GROUP))
            dv_p.append(vp.reshape(b, s, diff_heads, 2 * GROUP))
            dk_s.append(ks.reshape(nb, t_new, diff_heads, 2, GROUP))
            dv_s.append(vs.reshape(nb, t_new, diff_heads, 2 * GROUP))

        wr = _hi_lo(_pad_lanes(jnp.concatenate([moe_w_expert[i], moe_w_group[i]], axis=1)))
        br = _pad_lanes(_row(jnp.concatenate([moe_b_expert[i], moe_b_group[i]])))
        moe_args = (_row(ffn_norm[i]), wr, br, moe_w_gate[i].astype(BF16), moe_w_up[i].astype(BF16),
                    moe_w_down[i].astype(BF16))
        hp = _moe(hp, *moe_args, n_groups=n_groups)
        hs = _moe(hs, *moe_args, n_groups=n_groups)

        ple_args = (_row(ple_norm[i]), ple_w_gate[i].astype(BF16), ple_w_proj[i].astype(BF16))
        hp = _ple(hp, p_prompt[i].reshape(b * s, -1), *ple_args)
        hs = _ple(hs, p_sample[i].reshape(nb * t_new, -1), *ple_args)

    return (hp.reshape(b, s, d), hs.reshape(nb, t_new, d),
            jnp.stack(fk_p), jnp.stack(fv_p), jnp.stack(ff_p), jnp.stack(dk_p), jnp.stack(dv_p),
            jnp.stack(fk_s), jnp.stack(fv_s), jnp.stack(ff_s), jnp.stack(dk_s), jnp.stack(dv_s))
```

```python
import functools
import math

import jax
import jax.numpy as jnp
from jax import lax
from jax.experimental import pallas as pl
from jax.experimental.pallas import tpu as pltpu

F32 = jnp.float32
BF16 = jnp.bfloat16

LANES = 128
MXU_COLS = 256
GROUP = 64
VMEM_LIMIT_BYTES = 56 * 1024 * 1024
RMS_EPS = 1e-6
ROPE_THETA = 500000.0
ROPE_FRACTION = 4
N_MIXERS = 2
TOP_K = 2
LOG2E = 1.4426950408889634

PROJ_ROWS = 512
ATTN_ROWS = 1024
MOE_ROWS = 1024
DECODE_PAGES_PER_STEP = 8


def _params(*sem, flags=None):
    return pltpu.CompilerParams(dimension_semantics=sem, vmem_limit_bytes=VMEM_LIMIT_BYTES, flags=flags)


def _full(shape):
    zeros = (0,) * len(shape)
    return pl.BlockSpec(shape, lambda *_: zeros)


def _rows(tm, width):
    return pl.BlockSpec((tm, width), lambda i: (i, 0))


def _major_shape(n, width, seq):
    return jax.ShapeDtypeStruct((n // seq, width, seq), F32)


def _major_spec(width, tm, seq):
    tiles = seq // tm
    return pl.BlockSpec((None, width, tm), lambda i: (i // tiles, 0, i % tiles))


def _rms_rows(x, gain):
    return x * lax.rsqrt(jnp.mean(x * x, axis=-1, keepdims=True) + RMS_EPS) * gain


def _div_pow2(x, n):
    assert n & (n - 1) == 0
    return jnp.right_shift(x, n.bit_length() - 1)


def _split3(x):
    p1 = x.astype(BF16)
    r1 = x - p1.astype(F32)
    p2 = r1.astype(BF16)
    p3 = (r1 - p2.astype(F32)).astype(BF16)
    return p1, p2, p3


def _split2(x):
    hi = x.astype(BF16)
    return hi, (x - hi.astype(F32)).astype(BF16)


def _dot_pieces(xs, w_ref, cols=slice(None)):
    out = jnp.dot(xs[0], w_ref[0, :, cols], preferred_element_type=F32)
    if w_ref.shape[0] == 2:
        out = (out + jnp.dot(xs[1], w_ref[0, :, cols], preferred_element_type=F32)
               + jnp.dot(xs[0], w_ref[1, :, cols], preferred_element_type=F32))
    return out


def _group_scale(x, seg_ref, exp_ref, group):
    sq_hi, sq_lo = _split2(x * x)
    ss = (jnp.dot(sq_hi, seg_ref[...], preferred_element_type=F32)
          + jnp.dot(sq_lo, seg_ref[...], preferred_element_type=F32))
    hi, lo = _split2(lax.rsqrt(ss * (1.0 / group) + RMS_EPS))
    return jnp.dot(jnp.concatenate([hi, lo], axis=1), exp_ref[...], preferred_element_type=F32)


def _rope_tables(tile, tm, period, offset, invf):
    half = GROUP // ROPE_FRACTION // 2
    row = lax.broadcasted_iota(jnp.int32, (tm, LANES), 0) + tile * tm
    pos = (jnp.bitwise_and(row, period - 1) + offset).astype(F32)
    ang = pos * invf
    cs = jnp.cos(ang)
    sn = jnp.sin(ang)
    d = jnp.bitwise_and(lax.broadcasted_iota(jnp.int32, (tm, LANES), 1), GROUP - 1)
    return cs, jnp.where(d < half, -sn, 0.0), jnp.where(d >= half, sn, 0.0)


def _apply_rope(x, tables):
    cs, s_lo, s_hi = tables
    half = GROUP // ROPE_FRACTION // 2
    out = []
    for j in range(x.shape[1] // LANES):
        xs = x[:, j * LANES:(j + 1) * LANES]
        out.append(xs * cs + pltpu.roll(xs, LANES - half, 1) * s_lo + pltpu.roll(xs, half, 1) * s_hi)
    return jnp.concatenate(out, axis=1)


def _fox_proj_body(h_ref, an_ref, w_ref, wf_ref, bf_ref, qg_ref, kg_ref, seg_ref, exp_ref, tri_ref,
                   *rest, tm, d, nheads, period, prompt):
    if prompt:
        q_ref, kf_ref, vf_ref, lf_ref, c_ref, kb_ref, vb_ref, ct_ref, carry_ref = rest
    else:
        q_ref, kf_ref, vf_ref, lf_ref, c_ref, carry_ref = rest
    tile = pl.program_id(0)
    xn = _rms_rows(h_ref[...], an_ref[...])
    xb, x_lo = xs = _split2(xn)

    q = _dot_pieces(xs, w_ref, slice(0, d))
    q = q * _group_scale(q, seg_ref, exp_ref, GROUP) * qg_ref[...]
    k = _dot_pieces(xs, w_ref, slice(d, 2 * d))
    k = k * _group_scale(k, seg_ref, exp_ref, GROUP) * kg_ref[...]
    v = _dot_pieces(xs, w_ref, slice(2 * d, 3 * d))
    if prompt:
        q_ref[...] = (q * (GROUP ** -0.5 * LOG2E)).astype(BF16)
        kf_ref[...] = k.T
        vf_ref[...] = v.T
        kb_ref[...] = k.astype(BF16)
        vb_ref[...] = v.astype(BF16)
    else:
        q_ref[...] = q * GROUP ** -0.5
        kf_ref[...] = k
        vf_ref[...] = v

    z_hi = jnp.dot(xb, wf_ref[...], preferred_element_type=F32)
    z_lo = jnp.dot(x_lo, wf_ref[...], preferred_element_type=F32)
    z = z_hi + pltpu.roll(z_hi, LANES - nheads, 1) + z_lo + bf_ref[...]
    lf = jnp.minimum(z, 0.0) - jnp.log1p(jnp.exp(-jnp.abs(z)))
    lane = lax.broadcasted_iota(jnp.int32, (tm, LANES), 1)
    lf = jnp.where(lane < nheads, lf, 0.0)
    lf_ref[...] = lf.T[0:nheads, :] if prompt else lf[:, 0:nheads]

    p1, p2, p3 = _split3(lf)
    tri = tri_ref[...]
    c = (jnp.dot(tri, p1, preferred_element_type=F32) + jnp.dot(tri, p2, preferred_element_type=F32)
         + jnp.dot(tri, p3, preferred_element_type=F32))
    if period > tm:
        @pl.when(jnp.bitwise_and(tile * tm, period - 1) == 0)
        def _():
            carry_ref[...] = jnp.zeros_like(carry_ref)
        c = c + carry_ref[...]
        carry_ref[...] = c[tm - 1:tm, :]
    c_ref[...] = c[:, 0:nheads]
    if prompt:
        ct_ref[...] = c.T[0:nheads, :]


def _fox_project(h, an, w_qkv, wf, bf, qg, kg, seg, exp, tri, *, period, prompt):
    n, d = h.shape
    nheads = d // GROUP
    tm = min(PROJ_ROWS, n)
    assert n % tm == 0 and period & (period - 1) == 0 and (period % tm == 0 or tm % period == 0)
    row_f32 = jax.ShapeDtypeStruct((n, d), F32)
    row_bf16 = jax.ShapeDtypeStruct((n, d), BF16)
    head_f32 = jax.ShapeDtypeStruct((n, nheads), F32)
    if prompt:
        out_shape = [row_bf16, _major_shape(n, d, period), _major_shape(n, d, period),
                     _major_shape(n, nheads, period), head_f32,
                     row_bf16, row_bf16, jax.ShapeDtypeStruct((nheads, n), F32)]
        out_specs = [_rows(tm, d), _major_spec(d, tm, period), _major_spec(d, tm, period),
                     _major_spec(nheads, tm, period), _rows(tm, nheads),
                     _rows(tm, d), _rows(tm, d), pl.BlockSpec((nheads, tm), lambda i: (0, i))]
    else:
        out_shape = [row_f32, row_f32, row_f32, head_f32, head_f32]
        out_specs = [_rows(tm, d), _rows(tm, d), _rows(tm, d), _rows(tm, nheads), _rows(tm, nheads)]
    return pl.pallas_call(
        functools.partial(_fox_proj_body, tm=tm, d=d, nheads=nheads, period=period, prompt=prompt),
        grid=(n // tm,),
        in_specs=[_rows(tm, d), _full(an.shape), _full(w_qkv.shape), _full(wf.shape), _full(bf.shape),
                  _full(qg.shape), _full(kg.shape), _full(seg.shape), _full(exp.shape), _full(tri.shape)],
        out_specs=out_specs,
        out_shape=out_shape,
        scratch_shapes=[pltpu.VMEM((1, LANES), F32)],
        compiler_params=_params("arbitrary"),
        name="fox_project",
    )(h, an, w_qkv, wf, bf, qg, kg, seg, exp, tri)


def _diff_proj_body(h_ref, an_ref, w_ref, qg_ref, kg_ref, seg_ref, exp_ref, invf_ref, *rest,
                    tm, d, period, offset, prompt):
    if prompt:
        q_ref, kf_ref, vf_ref, kb_ref, vb_ref = rest
    else:
        q_ref, kf_ref, vf_ref = rest
    xb = _rms_rows(h_ref[...], an_ref[...]).astype(BF16)
    tables = _rope_tables(pl.program_id(0), tm, period, offset, invf_ref[...])

    q = jnp.dot(xb, w_ref[:, 0:d], preferred_element_type=F32)
    q = _apply_rope(q * _group_scale(q, seg_ref, exp_ref, GROUP) * qg_ref[...], tables)
    k = jnp.dot(xb, w_ref[:, d:2 * d], preferred_element_type=F32)
    k = _apply_rope(k * _group_scale(k, seg_ref, exp_ref, GROUP) * kg_ref[...], tables)
    v = jnp.dot(xb, w_ref[:, 2 * d:3 * d], preferred_element_type=F32)
    vf_ref[...] = v
    if prompt:
        q_ref[...] = (q * (GROUP ** -0.5 * LOG2E)).astype(BF16)
        kf_ref[...] = k.T
        kb_ref[...] = k.astype(BF16)
        vb_ref[...] = v.astype(BF16)
    else:
        q_ref[...] = (q * GROUP ** -0.5).astype(BF16)
        kf_ref[...] = k


def _diff_project(h, an, w_qkv, qg, kg, seg, exp, invf, *, period, offset, prompt):
    n, d = h.shape
    tm = min(PROJ_ROWS, n)
    assert n % tm == 0 and period & (period - 1) == 0
    row_f32 = jax.ShapeDtypeStruct((n, d), F32)
    row_bf16 = jax.ShapeDtypeStruct((n, d), BF16)
    n_out = 5 if prompt else 3
    return pl.pallas_call(
        functools.partial(_diff_proj_body, tm=tm, d=d, period=period, offset=offset, prompt=prompt),
        grid=(n // tm,),
        in_specs=[_rows(tm, d), _full(an.shape), _full(w_qkv.shape), _full(qg.shape), _full(kg.shape),
                  _full(seg.shape), _full(exp.shape), _full(invf.shape)],
        out_specs=[_rows(tm, d), _major_spec(d, tm, period) if prompt else _rows(tm, d)] + [_rows(tm, d)] * (n_out - 2),
        out_shape=[row_bf16, _major_shape(n, d, period) if prompt else row_f32, row_f32, row_bf16, row_bf16][:n_out],
        compiler_params=_params("parallel"),
        name="diff_project",
    )(h, an, w_qkv, qg, kg, seg, exp, invf)


def _diff_lambda(lam_ref, lam_init):
    lp = lam_ref[...]
    a = jnp.sum(lp[0:1, :] * lp[1:2, :], axis=1, keepdims=True)
    b = jnp.sum(lp[2:3, :] * lp[3:4, :], axis=1, keepdims=True)
    return jnp.exp(a) - jnp.exp(b) + lam_init


def _prompt_attn_body(*refs, fox, tq, lam_init):
    if fox:
        q_ref, k_ref, v_ref, c_ref, ct_ref, o_ref = refs
    else:
        q_ref, k_ref, v_ref, lam_ref, o_ref = refs
    col = pl.program_id(1)
    qi = pl.program_id(2)
    q2 = q_ref[...].astype(F32)
    lo = lax.broadcasted_iota(jnp.int32, (tq, LANES), 1) < GROUP
    qs = (jnp.where(lo, q2, 0.0).astype(BF16), jnp.where(lo, 0.0, q2).astype(BF16))
    if fox:
        c_tile = c_ref[...]
        head = lax.broadcasted_iota(jnp.int32, c_tile.shape, 1)
        cq = tuple(jnp.sum(jnp.where(head == 2 * col + s, c_tile, 0.0), axis=1, keepdims=True) * LOG2E
                   for s in (0, 1))
    causal = (lax.broadcasted_iota(jnp.int32, (tq, tq), 0) >= lax.broadcasted_iota(jnp.int32, (tq, tq), 1))

    def tile(kj, carry, diagonal):
        start = pl.multiple_of(kj * tq, tq)
        kt = k_ref[pl.ds(start, tq), :]
        vt = v_ref[pl.ds(start, tq), :]
        new = []
        for s in (0, 1):
            m, l, acc = carry[s]
            sc = lax.dot_general(qs[s], kt, (((1,), (1,)), ((), ())), preferred_element_type=F32)
            if fox:
                sc = sc + (cq[s] - ct_ref[pl.ds(2 * col + s, 1), pl.ds(start, tq)] * LOG2E)
            if diagonal:
                sc = jnp.where(causal, sc, -jnp.inf)
            m_new = jnp.maximum(m, jnp.max(sc, axis=1, keepdims=True))
            alpha = jnp.exp2(m - m_new)
            p = jnp.exp2(sc - m_new)
            l_new = alpha * l + jnp.sum(p, axis=1, keepdims=True)
            acc_new = alpha * acc + jnp.dot(p.astype(BF16), vt, preferred_element_type=F32)
            new.append((m_new, l_new, acc_new))
        return tuple(new)

    init = tuple((jnp.full((tq, 1), -jnp.inf, F32), jnp.zeros((tq, 1), F32), jnp.zeros((tq, LANES), F32))
                 for _ in (0, 1))
    carry = lax.fori_loop(0, qi, lambda kj, c: tile(kj, c, False), init)
    (_, l0, a0), (_, l1, a1) = tile(qi, carry, True)
    if fox:
        o_ref[...] = jnp.where(lo, a0 / l0, a1 / l1).astype(o_ref.dtype)
    else:
        o_ref[...] = (a0 / l0 - _diff_lambda(lam_ref, lam_init) * (a1 / l1)).astype(o_ref.dtype)


def _prompt_attention(q, k, v, extra, *, fox, lam_init=0.0):
    b, s, d = q.shape
    tq = min(ATTN_ROWS, s)
    assert s % tq == 0 and d % LANES == 0
    q_spec = pl.BlockSpec((None, tq, LANES), lambda bi, ci, qi: (bi, qi, ci))
    kv_spec = pl.BlockSpec((None, s, LANES), lambda bi, ci, qi: (bi, 0, ci))
    if fox:
        c, ct = extra
        nheads = c.shape[-1]
        extra_specs = [pl.BlockSpec((None, tq, nheads), lambda bi, ci, qi: (bi, qi, 0)),
                       pl.BlockSpec((nheads, s), lambda bi, ci, qi: (0, bi))]
    else:
        extra_specs = [_full(extra[0].shape)]
    return pl.pallas_call(
        functools.partial(_prompt_attn_body, fox=fox, tq=tq, lam_init=lam_init),
        grid=(b, d // LANES, s // tq),
        in_specs=[q_spec, kv_spec, kv_spec] + extra_specs,
        out_specs=q_spec,
        out_shape=jax.ShapeDtypeStruct((b, s, d), BF16 if fox else F32),
        compiler_params=_params("parallel", "parallel", "arbitrary"),
        name="fox_prompt_attention" if fox else "diff_prompt_attention",
    )(q, k, v, *extra)


def _decode_attn_body(pt_ref, *refs, fox, pages, t_new, ngroups, lam_init):
    del pt_ref
    q_ref, kn_ref, vn_ref, x_ref = refs[:4]
    k_refs = refs[4:4 + pages]
    v_refs = refs[4 + pages:4 + 2 * pages]
    pos = 4 + 2 * pages
    if fox:
        lf_refs = refs[pos:pos + pages]
        u_ref = refs[pos + pages]
        pos += pages + 1
    else:
        spread_ref = refs[pos]
        pos += 1
    o_ref, qbd_ref, m_ref, l_ref, acc_ref, last_ref = refs[pos:pos + 6]
    step = pl.program_id(1)
    rows = t_new * ngroups
    d = q_ref.shape[-1]
    own = (_div_pow2(lax.broadcasted_iota(jnp.int32, (ngroups, d), 1), GROUP)
           == lax.broadcasted_iota(jnp.int32, (ngroups, d), 0))

    @pl.when(step == 0)
    def _():
        q = q_ref[...].astype(F32)
        for t in range(t_new):
            row = jnp.where(own, jnp.broadcast_to(q[t:t + 1, :], (ngroups, d)), 0.0)
            if fox:
                hi, lo = _split2(row)
                qbd_ref[t * ngroups:(t + 1) * ngroups, :] = hi
                qbd_ref[rows + t * ngroups:rows + (t + 1) * ngroups, :] = lo
            else:
                qbd_ref[t * ngroups:(t + 1) * ngroups, :] = row.astype(BF16)
        m_ref[...] = jnp.full_like(m_ref, -jnp.inf)
        l_ref[...] = jnp.zeros_like(l_ref)
        acc_ref[...] = jnp.zeros_like(acc_ref)
        last_ref[...] = jnp.zeros_like(last_ref)

    carry_ref = last_ref

    def update(scs, pvs, target, others=()):
        m = m_ref[...]
        m_new = m
        for sc in scs:
            m_new = jnp.maximum(m_new, jnp.max(sc, axis=1, keepdims=True))
        alpha = jnp.exp(m - m_new)
        ps = [jnp.exp(sc - m_new) for sc in scs]
        l_ref[...] = alpha * l_ref[...] + sum(jnp.sum(p, axis=1, keepdims=True) for p in ps)
        target[...] = alpha * target[...] + sum(pv(p) for pv, p in zip(pvs, ps))
        for ref in others:
            ref[...] = alpha * ref[...]
        m_ref[...] = m_new

    def fold(x):
        return x[0:rows] + x[rows:2 * rows]

    qbd = qbd_ref[...]
    if not fox:
        spread = spread_ref[...]
        heads = spread.shape[1] // spread.shape[0]
        own_head = (jnp.bitwise_and(lax.broadcasted_iota(jnp.int32, (rows, spread.shape[1]), 1), heads - 1)
                    == _div_pow2(jnp.bitwise_and(lax.broadcasted_iota(jnp.int32, (rows, spread.shape[1]), 0),
                                                 ngroups - 1), ngroups // heads))
    scs, pvs = [], []
    if fox:
        carry = carry_ref[...]
    for g in range(pages):
        if fox:
            k_hi, k_lo = _split2(k_refs[g][...])
            sc = (fold(jnp.dot(qbd, k_hi, preferred_element_type=F32))
                  + jnp.dot(qbd[0:rows], k_lo, preferred_element_type=F32))
            u = u_ref[...]
            suf = sum(jnp.dot(p, u, preferred_element_type=F32) for p in _split3(lf_refs[g][...]))
            page = u.shape[0]
            decay = suf[:, 0:page] + carry
            carry = carry + suf[:, page:page + 1]
            scs.append(sc + (jnp.concatenate([decay] * t_new, axis=0) + x_ref[...]))
            by_key = (((1,), (1,)), ((), ()))

            def pv(p, v_ref=v_refs[g]):
                v_hi, v_lo = _split2(v_ref[...])
                p_hi, p_lo = _split2(p)
                both = jnp.concatenate([p_hi, p_lo], axis=0)
                return (fold(lax.dot_general(both, v_hi, by_key, preferred_element_type=F32))
                        + lax.dot_general(p_hi, v_lo, by_key, preferred_element_type=F32))
        else:
            scs.append(jnp.dot(qbd, k_refs[g][...].astype(BF16), preferred_element_type=F32))

            def pv(p, v_ref=v_refs[g]):
                wide = jnp.dot(p.astype(BF16), spread, preferred_element_type=F32)
                return jnp.dot(jnp.where(own_head, wide, 0.0).astype(BF16), v_ref[...].astype(BF16),
                               preferred_element_type=F32)
        pvs.append(pv)
    if fox:
        carry_ref[...] = carry
    update(scs, pvs, acc_ref)

    @pl.when(step == pl.num_programs(1) - 1)
    def _():
        qf = fold(qbd.astype(F32)) if fox else qbd.astype(F32)
        tok = _div_pow2(lax.broadcasted_iota(jnp.int32, (rows, 1), 0), ngroups)
        for t in range(t_new):
            sc = jnp.sum(qf * kn_ref[t:t + 1, :], axis=1, keepdims=True)
            if fox:
                cn = x_ref[...]
                sc = sc + (cn - jnp.concatenate([cn[t * ngroups:(t + 1) * ngroups, :]] * t_new, axis=0))
            sc = jnp.where(tok >= t, sc, -jnp.inf)
            if fox:
                update([sc], [lambda p, t=t: p * vn_ref[t:t + 1, :]], acc_ref)
            else:
                update([sc], [lambda p, t=t: p * vn_ref[t:t + 1, :]], last_ref, others=(acc_ref,))
        if fox:
            weighted = acc_ref[...] * (1.0 / l_ref[...])
            for t in range(t_new):
                blk = jnp.where(own, weighted[t * ngroups:(t + 1) * ngroups, :], 0.0)
                o_ref[t:t + 1, :] = jnp.sum(blk, axis=0, keepdims=True)
        else:
            sub = jnp.bitwise_and(lax.broadcasted_iota(jnp.int32, (rows, 1), 0), 1)
            coef = jnp.where(sub == 0, 1.0, -_diff_lambda(x_ref, lam_init)) / l_ref[...]
            own_lanes = (_div_pow2(lax.broadcasted_iota(jnp.int32, (ngroups, d), 1), 2 * GROUP)
                         == _div_pow2(lax.broadcasted_iota(jnp.int32, (ngroups, d), 0), 2))
            cached = acc_ref[...] * coef
            fresh = last_ref[...] * coef
            for t in range(t_new):
                r0 = t * ngroups
                blk = jnp.where(own_lanes, fresh[r0:r0 + ngroups, :], 0.0)
                pairs = [cached[r0 + 2 * h:r0 + 2 * h + 1, :] + cached[r0 + 2 * h + 1:r0 + 2 * h + 2, :]
                         for h in range(ngroups // 2)]
                o_ref[t:t + 1, :] = jnp.sum(blk, axis=0, keepdims=True) + jnp.concatenate(pairs, axis=1)


def _decode_attention(page_table, q, k_new, v_new, extra, k_cache, v_cache, lf_cache=None, *, fox, lam_init=0.0):
    nb, t_new, d = q.shape
    ngroups = d // GROUP
    rows = t_new * ngroups
    n_pages = page_table.shape[1]
    page = k_cache.shape[2]
    pages = math.gcd(DECODE_PAGES_PER_STEP, n_pages)
    steps = n_pages // pages

    def tok_spec():
        return pl.BlockSpec((None, t_new, d), lambda b, s, pt: (b, 0, 0))

    def page_spec(g, shape):
        zeros = (0,) * len(shape)
        return pl.BlockSpec((None,) + shape, lambda b, s, pt: (pt[b, n_pages - 1 - (s * pages + g)],) + zeros)

    if fox:
        x_spec = pl.BlockSpec((None, rows, 1), lambda b, s, pt: (b, 0, 0))
    else:
        x_spec = pl.BlockSpec(extra.shape, lambda b, s, pt: (0, 0))
    in_specs = [tok_spec(), tok_spec(), tok_spec(), x_spec]
    in_specs += [page_spec(g, k_cache.shape[1:]) for g in range(pages)]
    in_specs += [page_spec(g, v_cache.shape[1:]) for g in range(pages)]
    args = [q, k_new, v_new, extra] + [k_cache] * pages + [v_cache] * pages
    scratch = [pltpu.VMEM(((2 if fox else 1) * rows, d), BF16), pltpu.VMEM((rows, 1), F32),
               pltpu.VMEM((rows, 1), F32)]
    if fox:
        j = lax.broadcasted_iota(jnp.int32, (page, MXU_COLS), 0)
        i = lax.broadcasted_iota(jnp.int32, (page, MXU_COLS), 1)
        u = jnp.where((j > i) | (i >= page), 1.0, 0.0).astype(BF16)
        in_specs += [page_spec(g, lf_cache.shape[1:]) for g in range(pages)]
        in_specs += [pl.BlockSpec(u.shape, lambda b, s, pt: (0, 0))]
        args += [lf_cache] * pages + [u]
        scratch += [pltpu.VMEM((rows, d), F32), pltpu.VMEM((ngroups, 1), F32)]
    else:
        width = v_cache.shape[1]
        heads = width // page
        spread = (_div_pow2(lax.broadcasted_iota(jnp.int32, (page, width), 1), heads)
                  == lax.broadcasted_iota(jnp.int32, (page, width), 0)).astype(BF16)
        in_specs += [pl.BlockSpec(spread.shape, lambda b, s, pt: (0, 0))]
        args += [spread]
        scratch += [pltpu.VMEM((rows, v_cache.shape[2]), F32), pltpu.VMEM((rows, d), F32)]
    return pl.pallas_call(
        functools.partial(_decode_attn_body, fox=fox, pages=pages, t_new=t_new, ngroups=ngroups, lam_init=lam_init),
        grid_spec=pltpu.PrefetchScalarGridSpec(
            num_scalar_prefetch=1, grid=(nb, steps), in_specs=in_specs, out_specs=tok_spec(),
            scratch_shapes=scratch),
        out_shape=jax.ShapeDtypeStruct((nb, t_new, d), F32),
        compiler_params=_params("parallel", "arbitrary"),
        name="fox_decode_attention" if fox else "diff_decode_attention",
    )(page_table, *args)


def _out_proj_body(*refs, diff, out_scale):
    if diff:
        h_ref, o_ref, w_ref, g_ref, seg_ref, exp_ref, y_ref = refs
        o = o_ref[...].astype(F32)
        o = o * _group_scale(o, seg_ref, exp_ref, 2 * GROUP) * g_ref[...] * out_scale
    else:
        h_ref, o_ref, w_ref, y_ref = refs
        o = o_ref[...]
    pieces = _split2(o) if w_ref.shape[0] == 2 else (o.astype(BF16),)
    y_ref[...] = h_ref[...] + _dot_pieces(pieces, w_ref)


def _out_project(h, o, w, *norm, diff, out_scale=1.0):
    n, d = h.shape
    tm = min(PROJ_ROWS, n)
    assert n % tm == 0
    return pl.pallas_call(
        functools.partial(_out_proj_body, diff=diff, out_scale=out_scale),
        grid=(n // tm,),
        in_specs=[_rows(tm, d), _rows(tm, d), _full(w.shape)] + [_full(a.shape) for a in norm],
        out_specs=_rows(tm, d),
        out_shape=jax.ShapeDtypeStruct((n, d), F32),
        compiler_params=_params("parallel"),
        name="attn_out_project",
    )(h, o, w, *norm)


def _route(logits, n_experts, n_groups):
    per_group = n_experts // n_groups
    lane = lax.broadcasted_iota(jnp.int32, logits.shape, 1)
    neg = -jnp.inf
    big = jnp.int32(LANES)

    def lane_max(x):
        return jnp.max(x, axis=1, keepdims=True)

    def first_lane(mask):
        return jnp.min(jnp.where(mask, lane, big), axis=1, keepdims=True)

    is_group = (lane >= n_experts) & (lane < n_experts + n_groups)
    g_logit = jnp.where(is_group, logits, neg)
    g_max = lane_max(g_logit)
    g_sel = first_lane(g_logit == g_max) - n_experts
    g_weight = 1.0 / jnp.sum(jnp.exp(g_logit - g_max), axis=1, keepdims=True)

    chosen = (lane < n_experts) & (_div_pow2(lane, per_group) == g_sel)
    e_logit = jnp.where(chosen, logits, neg)
    e_exp = jnp.exp(e_logit - lane_max(e_logit))
    e_prob = e_exp / jnp.sum(e_exp, axis=1, keepdims=True)
    w1 = lane_max(e_prob)
    i1 = first_lane(chosen & (e_prob == w1))
    rest = chosen & (lane != i1)
    e_rest = jnp.where(rest, e_prob, -1.0)
    w2 = lane_max(e_rest)
    i2 = first_lane(rest & (e_rest == w2))
    norm = g_weight / (w1 + w2)
    return jnp.where(lane == i1, w1 * norm, 0.0) + jnp.where(lane == i2, w2 * norm, 0.0)


def _moe_body(h_ref, fn_ref, wr_ref, br_ref, wg_ref, wu_ref, wd_ref, y_ref, xn_ref, gate_ref, *,
              n_experts, n_groups):
    e = pl.program_id(1)

    @pl.when(e == 0)
    def _():
        h = h_ref[...]
        xn = _rms_rows(h, fn_ref[...])
        xb = xn.astype(BF16)
        xn_ref[...] = xb
        x_lo = (xn - xb.astype(F32)).astype(BF16)
        logits = (jnp.dot(xb, wr_ref[0], preferred_element_type=F32)
                  + jnp.dot(xb, wr_ref[1], preferred_element_type=F32)
                  + jnp.dot(x_lo, wr_ref[0], preferred_element_type=F32)) + br_ref[...]
        gate_ref[...] = _route(logits, n_experts, n_groups)
        y_ref[...] = h

    xb = xn_ref[...]
    gate = jnp.dot(xb, wg_ref[...], preferred_element_type=F32)
    up = jnp.dot(xb, wu_ref[...], preferred_element_type=F32)
    hid = (gate * jax.nn.sigmoid(gate) * up).astype(BF16)
    out = jnp.dot(hid, wd_ref[...], preferred_element_type=F32)
    lane = lax.broadcasted_iota(jnp.int32, gate_ref.shape, 1)
    w = jnp.sum(jnp.where(lane == e, gate_ref[...], 0.0), axis=1, keepdims=True)
    y_ref[...] += out * w


def _moe(h, fn, wr, br, wg, wu, wd, *, n_groups):
    n, d = h.shape
    n_experts, _, ff = wg.shape
    tm = min(MOE_ROWS, n)
    assert n % tm == 0
    return pl.pallas_call(
        functools.partial(_moe_body, n_experts=n_experts, n_groups=n_groups),
        grid=(n // tm, n_experts),
        in_specs=[pl.BlockSpec((tm, d), lambda i, e: (i, 0)),
                  pl.BlockSpec(fn.shape, lambda i, e: (0, 0)),
                  pl.BlockSpec(wr.shape, lambda i, e: (0, 0, 0)),
                  pl.BlockSpec(br.shape, lambda i, e: (0, 0)),
                  pl.BlockSpec((None, d, ff), lambda i, e: (e, 0, 0)),
                  pl.BlockSpec((None, d, ff), lambda i, e: (e, 0, 0)),
                  pl.BlockSpec((None, ff, d), lambda i, e: (e, 0, 0))],
        out_specs=pl.BlockSpec((tm, d), lambda i, e: (i, 0)),
        out_shape=jax.ShapeDtypeStruct((n, d), F32),
        scratch_shapes=[pltpu.VMEM((tm, d), BF16), pltpu.VMEM((tm, LANES), F32)],
        compiler_params=_params("parallel", "arbitrary"),
        name="hier_moe",
    )(h, fn, wr, br, wg, wu, wd)


def _ple_body(h_ref, p_ref, g_ref, wg_ref, wp_ref, y_ref):
    h = h_ref[...]
    xb = _rms_rows(h, g_ref[...]).astype(BF16)
    gate = jax.nn.sigmoid(jnp.dot(xb, wg_ref[...], preferred_element_type=F32))
    proj = jnp.dot(p_ref[...].astype(BF16), wp_ref[...], preferred_element_type=F32)
    y_ref[...] = h + gate * proj


def _ple(h, p, g, wg, wp):
    n, d = h.shape
    tm = min(PROJ_ROWS, n)
    assert n % tm == 0
    return pl.pallas_call(
        _ple_body,
        grid=(n // tm,),
        in_specs=[_rows(tm, d), _rows(tm, p.shape[1]), _full(g.shape), _full(wg.shape), _full(wp.shape)],
        out_specs=_rows(tm, d),
        out_shape=jax.ShapeDtypeStruct((n, d), F32),
        compiler_params=_params("parallel"),
        name="per_layer_embedding",
    )(h, p, g, wg, wp)


def _segment_tables(d, group):
    lane_group = jnp.arange(d)[:, None] // group
    seg = (lane_group == jnp.arange(LANES)[None, :]).astype(BF16)
    return seg, jnp.concatenate([seg.T, seg.T], axis=0)


def _cumsum_table(tm, period):
    r = jnp.arange(tm)[:, None]
    s = jnp.arange(tm)[None, :]
    return ((s <= r) & (r // period == s // period)).astype(BF16)


def _row(vec, reps=1):
    return jnp.tile(vec.astype(F32), reps)[None, :]


def _hi_lo(w):
    w = w.astype(F32)
    bits = lax.bitcast_convert_type(w, jnp.uint32)
    bits = (bits + jnp.uint32(0x7FFF) + ((bits >> 16) & jnp.uint32(1))) & jnp.uint32(0xFFFF0000)
    top = lax.bitcast_convert_type(bits, F32)
    return jnp.stack([top.astype(BF16), (w - top).astype(BF16)])


def _pad_lanes(x):
    return jnp.pad(x, [(0, 0)] * (x.ndim - 1) + [(0, LANES - x.shape[-1])])


def kernel(x_prompt, x_sample, p_prompt, p_sample, cache_fox_k, cache_fox_v, cache_fox_logf, cache_diff_k, cache_diff_v, page_table, attn_norm, fox_w_in, fox_b_f, fox_q_norm, fox_k_norm, fox_w_out, diff_w_in, diff_q_norm, diff_k_norm, diff_lambda, diff_subln, diff_w_out, ffn_norm, moe_w_group, moe_b_group, moe_w_expert, moe_b_expert, moe_w_gate, moe_w_up, moe_w_down, ple_norm, ple_w_gate, ple_w_proj):
    b, s, d = x_prompt.shape
    nb, t_new, _ = x_sample.shape
    depth = attn_norm.shape[0]
    n_pool, page = cache_fox_k.shape[1:3]
    past_len = page_table.shape[1] * page
    fox_heads = fox_b_f.shape[-1]
    diff_heads = d // (2 * GROUP)
    n_groups = moe_w_group.shape[-1]
    n_experts = moe_w_expert.shape[-1]
    assert fox_q_norm.shape[-1] == GROUP and diff_q_norm.shape[-1] == GROUP and fox_heads * GROUP == d
    assert 2 * fox_heads <= LANES and n_experts + n_groups <= LANES

    seg64, exp64 = _segment_tables(d, GROUP)
    seg128, exp128 = _segment_tables(d, 2 * GROUP)
    tri_p = _cumsum_table(min(PROJ_ROWS, b * s), s)
    tri_s = _cumsum_table(min(PROJ_ROWS, nb * t_new), t_new)
    rot = GROUP // ROPE_FRACTION
    inv_freq = jnp.power(jnp.float32(ROPE_THETA), -jnp.arange(rot // 2, dtype=jnp.float32) * (2.0 / rot))
    lane_d = jnp.arange(LANES) % GROUP
    invf = jnp.where(lane_d < rot, inv_freq[lane_d % (rot // 2)], 0.0)[None, :]

    fk_p, fv_p, ff_p, dk_p, dv_p = [], [], [], [], []
    fk_s, fv_s, ff_s, dk_s, dv_s = [], [], [], [], []
    hp = x_prompt.reshape(b * s, d)
    hs = x_sample.reshape(nb * t_new, d)
    for i in range(depth):
        j = i // N_MIXERS
        an = _row(attn_norm[i])
        if i % N_MIXERS == 0:
            w_qkv = _hi_lo(fox_w_in[j][:, :3 * d])
            wf = _pad_lanes(jnp.concatenate(list(_hi_lo(fox_w_in[j][:, 3 * d:])), axis=1))
            bf = _pad_lanes(_row(fox_b_f[j]))
            qg, kg = _row(fox_q_norm[j], fox_heads), _row(fox_k_norm[j], fox_heads)
            w_out = _hi_lo(fox_w_out[j])
            qp, kp, vp, lfp, cp, kbp, vbp, ctp = _fox_project(
                hp, an, w_qkv[:1], wf, bf, qg, kg, seg64, exp64, tri_p, period=s, prompt=True)
            qs, ks, vs, lfs, cs = _fox_project(
                hs, an, w_qkv, wf, bf, qg, kg, seg64, exp64, tri_s, period=t_new, prompt=False)
            op = _prompt_attention(qp.reshape(b, s, d), kbp.reshape(b, s, d), vbp.reshape(b, s, d),
                                   (cp.reshape(b, s, fox_heads), ctp), fox=True)
            os_ = _decode_attention(
                page_table, qs.reshape(nb, t_new, d), ks.reshape(nb, t_new, d), vs.reshape(nb, t_new, d),
                cs.reshape(nb, t_new * fox_heads, 1),
                jnp.transpose(cache_fox_k[j], (0, 2, 3, 1)).reshape(n_pool, d, page),
                jnp.transpose(cache_fox_v[j], (0, 2, 3, 1)).reshape(n_pool, d, page),
                jnp.transpose(cache_fox_logf[j], (0, 2, 1)), fox=True)
            hp = _out_project(hp, op.reshape(b * s, d), w_out[:1], diff=False)
            hs = _out_project(hs, os_.reshape(nb * t_new, d), w_out, diff=False)
            fk_p.append(jnp.transpose(kp.reshape(b, fox_heads, GROUP, s), (0, 3, 1, 2)))
            fv_p.append(jnp.transpose(vp.reshape(b, fox_heads, GROUP, s), (0, 3, 1, 2)))
            ff_p.append(jnp.transpose(lfp, (0, 2, 1)))
            fk_s.append(ks.reshape(nb, t_new, fox_heads, GROUP))
            fv_s.append(vs.reshape(nb, t_new, fox_heads, GROUP))
            ff_s.append(lfs.reshape(nb, t_new, fox_heads))
        else:
            lam_init = 0.8 - 0.6 * math.exp(-0.3 * i)
            w_qkv = diff_w_in[j].astype(BF16)
            qg, kg = _row(diff_q_norm[j], 2 * diff_heads), _row(diff_k_norm[j], 2 * diff_heads)
            w_out = diff_w_out[j].astype(BF16)[None]
            subln = _row(diff_subln[j], diff_heads)
            lam = diff_lambda[j].astype(F32)
            qp, kp, vp, kbp, vbp = _diff_project(hp, an, w_qkv, qg, kg, seg64, exp64, invf,
                                                 period=s, offset=0, prompt=True)
            qs, ks, vs = _diff_project(hs, an, w_qkv, qg, kg, seg64, exp64, invf,
                                       period=t_new, offset=past_len, prompt=False)
            op = _prompt_attention(qp.reshape(b, s, d), kbp.reshape(b, s, d), vbp.reshape(b, s, d),
                                   (lam,), fox=False, lam_init=lam_init)
            os_ = _decode_attention(
                page_table, qs.reshape(nb, t_new, d), ks.reshape(nb, t_new, d), vs.reshape(nb, t_new, d),
                lam, jnp.transpose(cache_diff_k[j], (0, 2, 3, 4, 1)).reshape(n_pool, d, page),
                cache_diff_v[j].reshape(n_pool, page * diff_heads, 2 * GROUP), fox=False, lam_init=lam_init)
            hp = _out_project(hp, op.reshape(b * s, d), w_out, subln, seg128, exp128,
                              diff=True, out_scale=1.0 - lam_init)
            hs = _out_project(hs, os_.reshape(nb * t_new, d), w_out, subln, seg128, exp128,
                              diff=True, out_scale=1.0 - lam_init)
            dk_p.append(jnp.transpose(kp.reshape(b, diff_heads, 2, GROUP, s), (0, 4, 1, 2, 3)))
            dv_p.append(vp.reshape(b, s, diff_heads, 2 * GROUP))
            dk_s.append(ks.reshape(nb, t_new, diff_heads, 2, GROUP))
            dv_s.append(vs.reshape(nb, t_new, diff_heads, 2 * GROUP))

        wr = _hi_lo(_pad_lanes(jnp.concatenate([moe_w_expert[i], moe_w_group[i]], axis=1)))
        br = _pad_lanes(_row(jnp.concatenate([moe_b_expert[i], moe_b_group[i]])))
        moe_args = (_row(ffn_norm[i]), wr, br, moe_w_gate[i].astype(BF16), moe_w_up[i].astype(BF16),
                    moe_w_down[i].astype(BF16))
        hp = _moe(hp, *moe_args, n_groups=n_groups)
        hs = _moe(hs, *moe_args, n_groups=n_groups)

        ple_args = (_row(ple_norm[i]), ple_w_gate[i].astype(BF16), ple_w_proj[i].astype(BF16))
        hp = _ple(hp, p_prompt[i].reshape(b * s, -1), *ple_args)
        hs = _ple(hs, p_sample[i].reshape(nb * t_new, -1), *ple_args)

    return (hp.reshape(b, s, d), hs.reshape(nb, t_new, d),
            jnp.stack(fk_p), jnp.stack(fv_p), jnp.stack(ff_p), jnp.stack(dk_p), jnp.stack(dv_p),
            jnp.stack(fk_s), jnp.stack(fv_s), jnp.stack(ff_s), jnp.stack(dk_s), jnp.stack(dv_s))
```

```python
import functools
import math

import jax
import jax.numpy as jnp
from jax import lax
from jax.experimental import pallas as pl
from jax.experimental.pallas import tpu as pltpu

F32 = jnp.float32
BF16 = jnp.bfloat16

LANES = 128
MXU_COLS = 256
GROUP = 64
VMEM_LIMIT_BYTES = 56 * 1024 * 1024
RMS_EPS = 1e-6
ROPE_THETA = 500000.0
ROPE_FRACTION = 4
N_MIXERS = 2
TOP_K = 2
LOG2E = 1.4426950408889634

PROJ_ROWS = 512
ATTN_ROWS = 1024
MOE_ROWS = 1024
DECODE_PAGES_PER_STEP = 16


def _params(*sem, flags=None):
    return pltpu.CompilerParams(dimension_semantics=sem, vmem_limit_bytes=VMEM_LIMIT_BYTES, flags=flags)


def _full(shape):
    zeros = (0,) * len(shape)
    return pl.BlockSpec(shape, lambda *_: zeros)


def _rows(tm, width):
    return pl.BlockSpec((tm, width), lambda i: (i, 0))


def _major_shape(n, width, seq):
    return jax.ShapeDtypeStruct((n // seq, width, seq), F32)


def _major_spec(width, tm, seq):
    tiles = seq // tm
    return pl.BlockSpec((None, width, tm), lambda i: (i // tiles, 0, i % tiles))


def _rms_rows(x, gain):
    return x * lax.rsqrt(jnp.mean(x * x, axis=-1, keepdims=True) + RMS_EPS) * gain


def _div_pow2(x, n):
    assert n & (n - 1) == 0
    return jnp.right_shift(x, n.bit_length() - 1)


def _split3(x):
    p1 = x.astype(BF16)
    r1 = x - p1.astype(F32)
    p2 = r1.astype(BF16)
    p3 = (r1 - p2.astype(F32)).astype(BF16)
    return p1, p2, p3


def _split2(x):
    hi = x.astype(BF16)
    return hi, (x - hi.astype(F32)).astype(BF16)


def _dot_pieces(xs, w_ref, cols=slice(None)):
    out = jnp.dot(xs[0], w_ref[0, :, cols], preferred_element_type=F32)
    if w_ref.shape[0] == 2:
        out = (out + jnp.dot(xs[1], w_ref[0, :, cols], preferred_element_type=F32)
               + jnp.dot(xs[0], w_ref[1, :, cols], preferred_element_type=F32))
    return out


def _group_scale(x, seg_ref, exp_ref, group):
    sq_hi, sq_lo = _split2(x * x)
    ss = (jnp.dot(sq_hi, seg_ref[...], preferred_element_type=F32)
          + jnp.dot(sq_lo, seg_ref[...], preferred_element_type=F32))
    hi, lo = _split2(lax.rsqrt(ss * (1.0 / group) + RMS_EPS))
    return jnp.dot(jnp.concatenate([hi, lo], axis=1), exp_ref[...], preferred_element_type=F32)


def _rope_tables(tile, tm, period, offset, invf):
    half = GROUP // ROPE_FRACTION // 2
    row = lax.broadcasted_iota(jnp.int32, (tm, LANES), 0) + tile * tm
    pos = (jnp.bitwise_and(row, period - 1) + offset).astype(F32)
    ang = pos * invf
    cs = jnp.cos(ang)
    sn = jnp.sin(ang)
    d = jnp.bitwise_and(lax.broadcasted_iota(jnp.int32, (tm, LANES), 1), GROUP - 1)
    return cs, jnp.where(d < half, -sn, 0.0), jnp.where(d >= half, sn, 0.0)


def _apply_rope(x, tables):
    cs, s_lo, s_hi = tables
    half = GROUP // ROPE_FRACTION // 2
    out = []
    for j in range(x.shape[1] // LANES):
        xs = x[:, j * LANES:(j + 1) * LANES]
        out.append(xs * cs + pltpu.roll(xs, LANES - half, 1) * s_lo + pltpu.roll(xs, half, 1) * s_hi)
    return jnp.concatenate(out, axis=1)


def _fox_proj_body(h_ref, an_ref, w_ref, wf_ref, bf_ref, qg_ref, kg_ref, seg_ref, exp_ref, tri_ref,
                   *rest, tm, d, nheads, period, prompt):
    if prompt:
        q_ref, kf_ref, vf_ref, lf_ref, c_ref, kb_ref, vb_ref, ct_ref, carry_ref = rest
    else:
        q_ref, kf_ref, vf_ref, lf_ref, c_ref, carry_ref = rest
    tile = pl.program_id(0)
    xn = _rms_rows(h_ref[...], an_ref[...])
    xb, x_lo = xs = _split2(xn)

    q = _dot_pieces(xs, w_ref, slice(0, d))
    q = q * _group_scale(q, seg_ref, exp_ref, GROUP) * qg_ref[...]
    k = _dot_pieces(xs, w_ref, slice(d, 2 * d))
    k = k * _group_scale(k, seg_ref, exp_ref, GROUP) * kg_ref[...]
    v = _dot_pieces(xs, w_ref, slice(2 * d, 3 * d))
    if prompt:
        q_ref[...] = (q * (GROUP ** -0.5 * LOG2E)).astype(BF16)
        kf_ref[...] = k.T
        vf_ref[...] = v.T
        kb_ref[...] = k.astype(BF16)
        vb_ref[...] = v.astype(BF16)
    else:
        q_ref[...] = q * GROUP ** -0.5
        kf_ref[...] = k
        vf_ref[...] = v

    z_hi = jnp.dot(xb, wf_ref[...], preferred_element_type=F32)
    z_lo = jnp.dot(x_lo, wf_ref[...], preferred_element_type=F32)
    z = z_hi + pltpu.roll(z_hi, LANES - nheads, 1) + z_lo + bf_ref[...]
    lf = jnp.minimum(z, 0.0) - jnp.log1p(jnp.exp(-jnp.abs(z)))
    lane = lax.broadcasted_iota(jnp.int32, (tm, LANES), 1)
    lf = jnp.where(lane < nheads, lf, 0.0)
    lf_ref[...] = lf.T[0:nheads, :] if prompt else lf[:, 0:nheads]

    p1, p2, p3 = _split3(lf)
    tri = tri_ref[...]
    c = (jnp.dot(tri, p1, preferred_element_type=F32) + jnp.dot(tri, p2, preferred_element_type=F32)
         + jnp.dot(tri, p3, preferred_element_type=F32))
    if period > tm:
        @pl.when(jnp.bitwise_and(tile * tm, period - 1) == 0)
        def _():
            carry_ref[...] = jnp.zeros_like(carry_ref)
        c = c + carry_ref[...]
        carry_ref[...] = c[tm - 1:tm, :]
    c_ref[...] = c[:, 0:nheads]
    if prompt:
        ct_ref[...] = c.T[0:nheads, :]


def _fox_project(h, an, w_qkv, wf, bf, qg, kg, seg, exp, tri, *, period, prompt):
    n, d = h.shape
    nheads = d // GROUP
    tm = min(PROJ_ROWS, n)
    assert n % tm == 0 and period & (period - 1) == 0 and (period % tm == 0 or tm % period == 0)
    row_f32 = jax.ShapeDtypeStruct((n, d), F32)
    row_bf16 = jax.ShapeDtypeStruct((n, d), BF16)
    head_f32 = jax.ShapeDtypeStruct((n, nheads), F32)
    if prompt:
        out_shape = [row_bf16, _major_shape(n, d, period), _major_shape(n, d, period),
                     _major_shape(n, nheads, period), head_f32,
                     row_bf16, row_bf16, jax.ShapeDtypeStruct((nheads, n), F32)]
        out_specs = [_rows(tm, d), _major_spec(d, tm, period), _major_spec(d, tm, period),
                     _major_spec(nheads, tm, period), _rows(tm, nheads),
                     _rows(tm, d), _rows(tm, d), pl.BlockSpec((nheads, tm), lambda i: (0, i))]
    else:
        out_shape = [row_f32, row_f32, row_f32, head_f32, head_f32]
        out_specs = [_rows(tm, d), _rows(tm, d), _rows(tm, d), _rows(tm, nheads), _rows(tm, nheads)]
    return pl.pallas_call(
        functools.partial(_fox_proj_body, tm=tm, d=d, nheads=nheads, period=period, prompt=prompt),
        grid=(n // tm,),
        in_specs=[_rows(tm, d), _full(an.shape), _full(w_qkv.shape), _full(wf.shape), _full(bf.shape),
                  _full(qg.shape), _full(kg.shape), _full(seg.shape), _full(exp.shape), _full(tri.shape)],
        out_specs=out_specs,
        out_shape=out_shape,
        scratch_shapes=[pltpu.VMEM((1, LANES), F32)],
        compiler_params=_params("arbitrary"),
        name="fox_project",
    )(h, an, w_qkv, wf, bf, qg, kg, seg, exp, tri)


def _diff_proj_body(h_ref, an_ref, w_ref, qg_ref, kg_ref, seg_ref, exp_ref, invf_ref, *rest,
                    tm, d, period, offset, prompt):
    if prompt:
        q_ref, kf_ref, vf_ref, kb_ref, vb_ref = rest
    else:
        q_ref, kf_ref, vf_ref = rest
    xb = _rms_rows(h_ref[...], an_ref[...]).astype(BF16)
    tables = _rope_tables(pl.program_id(0), tm, period, offset, invf_ref[...])

    q = jnp.dot(xb, w_ref[:, 0:d], preferred_element_type=F32)
    q = _apply_rope(q * _group_scale(q, seg_ref, exp_ref, GROUP) * qg_ref[...], tables)
    k = jnp.dot(xb, w_ref[:, d:2 * d], preferred_element_type=F32)
    k = _apply_rope(k * _group_scale(k, seg_ref, exp_ref, GROUP) * kg_ref[...], tables)
    v = jnp.dot(xb, w_ref[:, 2 * d:3 * d], preferred_element_type=F32)
    vf_ref[...] = v
    if prompt:
        q_ref[...] = (q * (GROUP ** -0.5 * LOG2E)).astype(BF16)
        kf_ref[...] = k.T
        kb_ref[...] = k.astype(BF16)
        vb_ref[...] = v.astype(BF16)
    else:
        q_ref[...] = (q * GROUP ** -0.5).astype(BF16)
        kf_ref[...] = k


def _diff_project(h, an, w_qkv, qg, kg, seg, exp, invf, *, period, offset, prompt):
    n, d = h.shape
    tm = min(PROJ_ROWS, n)
    assert n % tm == 0 and period & (period - 1) == 0
    row_f32 = jax.ShapeDtypeStruct((n, d), F32)
    row_bf16 = jax.ShapeDtypeStruct((n, d), BF16)
    n_out = 5 if prompt else 3
    return pl.pallas_call(
        functools.partial(_diff_proj_body, tm=tm, d=d, period=period, offset=offset, prompt=prompt),
        grid=(n // tm,),
        in_specs=[_rows(tm, d), _full(an.shape), _full(w_qkv.shape), _full(qg.shape), _full(kg.shape),
                  _full(seg.shape), _full(exp.shape), _full(invf.shape)],
        out_specs=[_rows(tm, d), _major_spec(d, tm, period) if prompt else _rows(tm, d)] + [_rows(tm, d)] * (n_out - 2),
        out_shape=[row_bf16, _major_shape(n, d, period) if prompt else row_f32, row_f32, row_bf16, row_bf16][:n_out],
        compiler_params=_params("parallel"),
        name="diff_project",
    )(h, an, w_qkv, qg, kg, seg, exp, invf)


def _diff_lambda(lam_ref, lam_init):
    lp = lam_ref[...]
    a = jnp.sum(lp[0:1, :] * lp[1:2, :], axis=1, keepdims=True)
    b = jnp.sum(lp[2:3, :] * lp[3:4, :], axis=1, keepdims=True)
    return jnp.exp(a) - jnp.exp(b) + lam_init


def _prompt_attn_body(*refs, fox, tq, lam_init):
    if fox:
        q_ref, k_ref, v_ref, c_ref, ct_ref, o_ref = refs
    else:
        q_ref, k_ref, v_ref, lam_ref, o_ref = refs
    col = pl.program_id(1)
    qi = pl.program_id(2)
    q2 = q_ref[...].astype(F32)
    lo = lax.broadcasted_iota(jnp.int32, (tq, LANES), 1) < GROUP
    qs = (jnp.where(lo, q2, 0.0).astype(BF16), jnp.where(lo, 0.0, q2).astype(BF16))
    if fox:
        c_tile = c_ref[...]
        head = lax.broadcasted_iota(jnp.int32, c_tile.shape, 1)
        cq = tuple(jnp.sum(jnp.where(head == 2 * col + s, c_tile, 0.0), axis=1, keepdims=True) * LOG2E
                   for s in (0, 1))
    causal = (lax.broadcasted_iota(jnp.int32, (tq, tq), 0) >= lax.broadcasted_iota(jnp.int32, (tq, tq), 1))

    def tile(kj, carry, diagonal):
        start = pl.multiple_of(kj * tq, tq)
        kt = k_ref[pl.ds(start, tq), :]
        vt = v_ref[pl.ds(start, tq), :]
        new = []
        for s in (0, 1):
            m, l, acc = carry[s]
            sc = lax.dot_general(qs[s], kt, (((1,), (1,)), ((), ())), preferred_element_type=F32)
            if fox:
                sc = sc + (cq[s] - ct_ref[pl.ds(2 * col + s, 1), pl.ds(start, tq)] * LOG2E)
            if diagonal:
                sc = jnp.where(causal, sc, -jnp.inf)
            m_new = jnp.maximum(m, jnp.max(sc, axis=1, keepdims=True))
            alpha = jnp.exp2(m - m_new)
            p = jnp.exp2(sc - m_new)
            l_new = alpha * l + jnp.sum(p, axis=1, keepdims=True)
            acc_new = alpha * acc + jnp.dot(p.astype(BF16), vt, preferred_element_type=F32)
            new.append((m_new, l_new, acc_new))
        return tuple(new)

    init = tuple((jnp.full((tq, 1), -jnp.inf, F32), jnp.zeros((tq, 1), F32), jnp.zeros((tq, LANES), F32))
                 for _ in (0, 1))
    carry = lax.fori_loop(0, qi, lambda kj, c: tile(kj, c, False), init)
    (_, l0, a0), (_, l1, a1) = tile(qi, carry, True)
    if fox:
        o_ref[...] = jnp.where(lo, a0 / l0, a1 / l1).astype(o_ref.dtype)
    else:
        o_ref[...] = (a0 / l0 - _diff_lambda(lam_ref, lam_init) * (a1 / l1)).astype(o_ref.dtype)


def _prompt_attention(q, k, v, extra, *, fox, lam_init=0.0):
    b, s, d = q.shape
    tq = min(ATTN_ROWS, s)
    assert s % tq == 0 and d % LANES == 0
    q_spec = pl.BlockSpec((None, tq, LANES), lambda bi, ci, qi: (bi, qi, ci))
    kv_spec = pl.BlockSpec((None, s, LANES), lambda bi, ci, qi: (bi, 0, ci))
    if fox:
        c, ct = extra
        nheads = c.shape[-1]
        extra_specs = [pl.BlockSpec((None, tq, nheads), lambda bi, ci, qi: (bi, qi, 0)),
                       pl.BlockSpec((nheads, s), lambda bi, ci, qi: (0, bi))]
    else:
        extra_specs = [_full(extra[0].shape)]
    return pl.pallas_call(
        functools.partial(_prompt_attn_body, fox=fox, tq=tq, lam_init=lam_init),
        grid=(b, d // LANES, s // tq),
        in_specs=[q_spec, kv_spec, kv_spec] + extra_specs,
        out_specs=q_spec,
        out_shape=jax.ShapeDtypeStruct((b, s, d), BF16 if fox else F32),
        compiler_params=_params("parallel", "parallel", "arbitrary"),
        name="fox_prompt_attention" if fox else "diff_prompt_attention",
    )(q, k, v, *extra)


def _decode_attn_body(pt_ref, *refs, fox, pages, t_new, ngroups, lam_init):
    del pt_ref
    q_ref, kn_ref, vn_ref, x_ref = refs[:4]
    k_refs = refs[4:4 + pages]
    v_refs = refs[4 + pages:4 + 2 * pages]
    pos = 4 + 2 * pages
    if fox:
        lf_refs = refs[pos:pos + pages]
        u_ref = refs[pos + pages]
        pos += pages + 1
    else:
        spread_ref = refs[pos]
        pos += 1
    o_ref, qbd_ref, m_ref, l_ref, acc_ref, last_ref = refs[pos:pos + 6]
    step = pl.program_id(1)
    rows = t_new * ngroups
    d = q_ref.shape[-1]
    own = (_div_pow2(lax.broadcasted_iota(jnp.int32, (ngroups, d), 1), GROUP)
           == lax.broadcasted_iota(jnp.int32, (ngroups, d), 0))

    @pl.when(step == 0)
    def _():
        q = q_ref[...].astype(F32)
        for t in range(t_new):
            row = jnp.where(own, jnp.broadcast_to(q[t:t + 1, :], (ngroups, d)), 0.0)
            if fox:
                hi, lo = _split2(row)
                qbd_ref[t * ngroups:(t + 1) * ngroups, :] = hi
                qbd_ref[rows + t * ngroups:rows + (t + 1) * ngroups, :] = lo
            else:
                qbd_ref[t * ngroups:(t + 1) * ngroups, :] = row.astype(BF16)
        m_ref[...] = jnp.full_like(m_ref, -jnp.inf)
        l_ref[...] = jnp.zeros_like(l_ref)
        acc_ref[...] = jnp.zeros_like(acc_ref)
        last_ref[...] = jnp.zeros_like(last_ref)

    carry_ref = last_ref

    def update(scs, pvs, target, others=()):
        m = m_ref[...]
        m_new = m
        for sc in scs:
            m_new = jnp.maximum(m_new, jnp.max(sc, axis=1, keepdims=True))
        alpha = jnp.exp(m - m_new)
        ps = [jnp.exp(sc - m_new) for sc in scs]
        l_ref[...] = alpha * l_ref[...] + sum(jnp.sum(p, axis=1, keepdims=True) for p in ps)
        target[...] = alpha * target[...] + sum(pv(p) for pv, p in zip(pvs, ps))
        for ref in others:
            ref[...] = alpha * ref[...]
        m_ref[...] = m_new

    def fold(x):
        return x[0:rows] + x[rows:2 * rows]

    qbd = qbd_ref[...]
    if not fox:
        spread = spread_ref[...]
        heads = spread.shape[1] // spread.shape[0]
        own_head = (jnp.bitwise_and(lax.broadcasted_iota(jnp.int32, (rows, spread.shape[1]), 1), heads - 1)
                    == _div_pow2(jnp.bitwise_and(lax.broadcasted_iota(jnp.int32, (rows, spread.shape[1]), 0),
                                                 ngroups - 1), ngroups // heads))
    scs, pvs = [], []
    if fox:
        carry = carry_ref[...]
    for g in range(pages):
        if fox:
            k_hi, k_lo = _split2(k_refs[g][...])
            sc = (fold(jnp.dot(qbd, k_hi, preferred_element_type=F32))
                  + jnp.dot(qbd[0:rows], k_lo, preferred_element_type=F32))
            u = u_ref[...]
            suf = sum(jnp.dot(p, u, preferred_element_type=F32) for p in _split3(lf_refs[g][...]))
            page = u.shape[0]
            decay = suf[:, 0:page] + carry
            carry = carry + suf[:, page:page + 1]
            scs.append(sc + (jnp.concatenate([decay] * t_new, axis=0) + x_ref[...]))
            def pv(p, v_ref=v_refs[g]):
                v_hi, v_lo = _split2(v_ref[...].T)
                p_hi, p_lo = _split2(p)
                both = jnp.concatenate([p_hi, p_lo], axis=0)
                return (fold(jnp.dot(both, v_hi, preferred_element_type=F32))
                        + jnp.dot(p_hi, v_lo, preferred_element_type=F32))
        else:
            scs.append(jnp.dot(qbd, k_refs[g][...].astype(BF16), preferred_element_type=F32))

            def pv(p, v_ref=v_refs[g]):
                wide = jnp.dot(p.astype(BF16), spread, preferred_element_type=F32)
                return jnp.dot(jnp.where(own_head, wide, 0.0).astype(BF16), v_ref[...].astype(BF16),
                               preferred_element_type=F32)
        pvs.append(pv)
    if fox:
        carry_ref[...] = carry
    update(scs, pvs, acc_ref)

    @pl.when(step == pl.num_programs(1) - 1)
    def _():
        qf = fold(qbd.astype(F32)) if fox else qbd.astype(F32)
        tok = _div_pow2(lax.broadcasted_iota(jnp.int32, (rows, 1), 0), ngroups)
        for t in range(t_new):
            sc = jnp.sum(qf * kn_ref[t:t + 1, :], axis=1, keepdims=True)
            if fox:
                cn = x_ref[...]
                sc = sc + (cn - jnp.concatenate([cn[t * ngroups:(t + 1) * ngroups, :]] * t_new, axis=0))
            sc = jnp.where(tok >= t, sc, -jnp.inf)
            if fox:
                update([sc], [lambda p, t=t: p * vn_ref[t:t + 1, :]], acc_ref)
            else:
                update([sc], [lambda p, t=t: p * vn_ref[t:t + 1, :]], last_ref, others=(acc_ref,))
        if fox:
            weighted = acc_ref[...] * (1.0 / l_ref[...])
            for t in range(t_new):
                blk = jnp.where(own, weighted[t * ngroups:(t + 1) * ngroups, :], 0.0)
                o_ref[t:t + 1, :] = jnp.sum(blk, axis=0, keepdims=True)
        else:
            sub = jnp.bitwise_and(lax.broadcasted_iota(jnp.int32, (rows, 1), 0), 1)
            coef = jnp.where(sub == 0, 1.0, -_diff_lambda(x_ref, lam_init)) / l_ref[...]
            own_lanes = (_div_pow2(lax.broadcasted_iota(jnp.int32, (ngroups, d), 1), 2 * GROUP)
                         == _div_pow2(lax.broadcasted_iota(jnp.int32, (ngroups, d), 0), 2))
            cached = acc_ref[...] * coef
            fresh = last_ref[...] * coef
            for t in range(t_new):
                r0 = t * ngroups
                blk = jnp.where(own_lanes, fresh[r0:r0 + ngroups, :], 0.0)
                pairs = [cached[r0 + 2 * h:r0 + 2 * h + 1, :] + cached[r0 + 2 * h + 1:r0 + 2 * h + 2, :]
                         for h in range(ngroups // 2)]
                o_ref[t:t + 1, :] = jnp.sum(blk, axis=0, keepdims=True) + jnp.concatenate(pairs, axis=1)


def _decode_attention(page_table, q, k_new, v_new, extra, k_cache, v_cache, lf_cache=None, *, fox, lam_init=0.0):
    nb, t_new, d = q.shape
    ngroups = d // GROUP
    rows = t_new * ngroups
    n_pages = page_table.shape[1]
    page = k_cache.shape[2]
    pages = math.gcd(DECODE_PAGES_PER_STEP, n_pages)
    steps = n_pages // pages

    def tok_spec():
        return pl.BlockSpec((None, t_new, d), lambda b, s, pt: (b, 0, 0))

    def page_spec(g, shape):
        zeros = (0,) * len(shape)
        return pl.BlockSpec((None,) + shape, lambda b, s, pt: (pt[b, n_pages - 1 - (s * pages + g)],) + zeros)

    if fox:
        x_spec = pl.BlockSpec((None, rows, 1), lambda b, s, pt: (b, 0, 0))
    else:
        x_spec = pl.BlockSpec(extra.shape, lambda b, s, pt: (0, 0))
    in_specs = [tok_spec(), tok_spec(), tok_spec(), x_spec]
    in_specs += [page_spec(g, k_cache.shape[1:]) for g in range(pages)]
    in_specs += [page_spec(g, v_cache.shape[1:]) for g in range(pages)]
    args = [q, k_new, v_new, extra] + [k_cache] * pages + [v_cache] * pages
    scratch = [pltpu.VMEM(((2 if fox else 1) * rows, d), BF16), pltpu.VMEM((rows, 1), F32),
               pltpu.VMEM((rows, 1), F32)]
    if fox:
        j = lax.broadcasted_iota(jnp.int32, (page, MXU_COLS), 0)
        i = lax.broadcasted_iota(jnp.int32, (page, MXU_COLS), 1)
        u = jnp.where((j > i) | (i >= page), 1.0, 0.0).astype(BF16)
        in_specs += [page_spec(g, lf_cache.shape[1:]) for g in range(pages)]
        in_specs += [pl.BlockSpec(u.shape, lambda b, s, pt: (0, 0))]
        args += [lf_cache] * pages + [u]
        scratch += [pltpu.VMEM((rows, d), F32), pltpu.VMEM((ngroups, 1), F32)]
    else:
        width = v_cache.shape[1]
        heads = width // page
        spread = (_div_pow2(lax.broadcasted_iota(jnp.int32, (page, width), 1), heads)
                  == lax.broadcasted_iota(jnp.int32, (page, width), 0)).astype(BF16)
        in_specs += [pl.BlockSpec(spread.shape, lambda b, s, pt: (0, 0))]
        args += [spread]
        scratch += [pltpu.VMEM((rows, v_cache.shape[2]), F32), pltpu.VMEM((rows, d), F32)]
    return pl.pallas_call(
        functools.partial(_decode_attn_body, fox=fox, pages=pages, t_new=t_new, ngroups=ngroups, lam_init=lam_init),
        grid_spec=pltpu.PrefetchScalarGridSpec(
            num_scalar_prefetch=1, grid=(nb, steps), in_specs=in_specs, out_specs=tok_spec(),
            scratch_shapes=scratch),
        out_shape=jax.ShapeDtypeStruct((nb, t_new, d), F32),
        compiler_params=_params("parallel", "arbitrary"),
        name="fox_decode_attention" if fox else "diff_decode_attention",
    )(page_table, *args)


def _out_proj_body(*refs, diff, out_scale):
    if diff:
        h_ref, o_ref, w_ref, g_ref, seg_ref, exp_ref, y_ref = refs
        o = o_ref[...].astype(F32)
        o = o * _group_scale(o, seg_ref, exp_ref, 2 * GROUP) * g_ref[...] * out_scale
    else:
        h_ref, o_ref, w_ref, y_ref = refs
        o = o_ref[...]
    pieces = _split2(o) if w_ref.shape[0] == 2 else (o.astype(BF16),)
    y_ref[...] = h_ref[...] + _dot_pieces(pieces, w_ref)


def _out_project(h, o, w, *norm, diff, out_scale=1.0):
    n, d = h.shape
    tm = min(PROJ_ROWS, n)
    assert n % tm == 0
    return pl.pallas_call(
        functools.partial(_out_proj_body, diff=diff, out_scale=out_scale),
        grid=(n // tm,),
        in_specs=[_rows(tm, d), _rows(tm, d), _full(w.shape)] + [_full(a.shape) for a in norm],
        out_specs=_rows(tm, d),
        out_shape=jax.ShapeDtypeStruct((n, d), F32),
        compiler_params=_params("parallel"),
        name="attn_out_project",
    )(h, o, w, *norm)


def _route(logits, n_experts, n_groups):
    per_group = n_experts // n_groups
    lane = lax.broadcasted_iota(jnp.int32, logits.shape, 1)
    neg = -jnp.inf
    big = jnp.int32(LANES)

    def lane_max(x):
        return jnp.max(x, axis=1, keepdims=True)

    def first_lane(mask):
        return jnp.min(jnp.where(mask, lane, big), axis=1, keepdims=True)

    is_group = (lane >= n_experts) & (lane < n_experts + n_groups)
    g_logit = jnp.where(is_group, logits, neg)
    g_max = lane_max(g_logit)
    g_sel = first_lane(g_logit == g_max) - n_experts
    g_weight = 1.0 / jnp.sum(jnp.exp(g_logit - g_max), axis=1, keepdims=True)

    chosen = (lane < n_experts) & (_div_pow2(lane, per_group) == g_sel)
    e_logit = jnp.where(chosen, logits, neg)
    e_exp = jnp.exp(e_logit - lane_max(e_logit))
    e_prob = e_exp / jnp.sum(e_exp, axis=1, keepdims=True)
    w1 = lane_max(e_prob)
    i1 = first_lane(chosen & (e_prob == w1))
    rest = chosen & (lane != i1)
    e_rest = jnp.where(rest, e_prob, -1.0)
    w2 = lane_max(e_rest)
    i2 = first_lane(rest & (e_rest == w2))
    norm = g_weight / (w1 + w2)
    return jnp.where(lane == i1, w1 * norm, 0.0) + jnp.where(lane == i2, w2 * norm, 0.0)


def _moe_body(h_ref, fn_ref, wr_ref, br_ref, wg_ref, wu_ref, wd_ref, y_ref, xn_ref, gate_ref, *,
              n_experts, n_groups):
    e = pl.program_id(1)

    @pl.when(e == 0)
    def _():
        h = h_ref[...]
        xn = _rms_rows(h, fn_ref[...])
        xb = xn.astype(BF16)
        xn_ref[...] = xb
        x_lo = (xn - xb.astype(F32)).astype(BF16)
        logits = (jnp.dot(xb, wr_ref[0], preferred_element_type=F32)
                  + jnp.dot(xb, wr_ref[1], preferred_element_type=F32)
                  + jnp.dot(x_lo, wr_ref[0], preferred_element_type=F32)) + br_ref[...]
        gate_ref[...] = _route(logits, n_experts, n_groups)
        y_ref[...] = h

    xb = xn_ref[...]
    gate = jnp.dot(xb, wg_ref[...], preferred_element_type=F32)
    up = jnp.dot(xb, wu_ref[...], preferred_element_type=F32)
    hid = (gate * jax.nn.sigmoid(gate) * up).astype(BF16)
    out = jnp.dot(hid, wd_ref[...], preferred_element_type=F32)
    lane = lax.broadcasted_iota(jnp.int32, gate_ref.shape, 1)
    w = jnp.sum(jnp.where(lane == e, gate_ref[...], 0.0), axis=1, keepdims=True)
    y_ref[...] += out * w


def _moe(h, fn, wr, br, wg, wu, wd, *, n_groups):
    n, d = h.shape
    n_experts, _, ff = wg.shape
    tm = min(MOE_ROWS, n)
    assert n % tm == 0
    return pl.pallas_call(
        functools.partial(_moe_body, n_experts=n_experts, n_groups=n_groups),
        grid=(n // tm, n_experts),
        in_specs=[pl.BlockSpec((tm, d), lambda i, e: (i, 0)),
                  pl.BlockSpec(fn.shape, lambda i, e: (0, 0)),
                  pl.BlockSpec(wr.shape, lambda i, e: (0, 0, 0)),
                  pl.BlockSpec(br.shape, lambda i, e: (0, 0)),
                  pl.BlockSpec((None, d, ff), lambda i, e: (e, 0, 0)),
                  pl.BlockSpec((None, d, ff), lambda i, e: (e, 0, 0)),
                  pl.BlockSpec((None, ff, d), lambda i, e: (e, 0, 0))],
        out_specs=pl.BlockSpec((tm, d), lambda i, e: (i, 0)),
        out_shape=jax.ShapeDtypeStruct((n, d), F32),
        scratch_shapes=[pltpu.VMEM((tm, d), BF16), pltpu.VMEM((tm, LANES), F32)],
        compiler_params=_params("parallel", "arbitrary"),
        name="hier_moe",
    )(h, fn, wr, br, wg, wu, wd)


def _ple_body(h_ref, p_ref, g_ref, wg_ref, wp_ref, y_ref):
    h = h_ref[...]
    xb = _rms_rows(h, g_ref[...]).astype(BF16)
    gate = jax.nn.sigmoid(jnp.dot(xb, wg_ref[...], preferred_element_type=F32))
    proj = jnp.dot(p_ref[...].astype(BF16), wp_ref[...], preferred_element_type=F32)
    y_ref[...] = h + gate * proj


def _ple(h, p, g, wg, wp):
    n, d = h.shape
    tm = min(PROJ_ROWS, n)
    assert n % tm == 0
    return pl.pallas_call(
        _ple_body,
        grid=(n // tm,),
        in_specs=[_rows(tm, d), _rows(tm, p.shape[1]), _full(g.shape), _full(wg.shape), _full(wp.shape)],
        out_specs=_rows(tm, d),
        out_shape=jax.ShapeDtypeStruct((n, d), F32),
        compiler_params=_params("parallel"),
        name="per_layer_embedding",
    )(h, p, g, wg, wp)


def _segment_tables(d, group):
    lane_group = jnp.arange(d)[:, None] // group
    seg = (lane_group == jnp.arange(LANES)[None, :]).astype(BF16)
    return seg, jnp.concatenate([seg.T, seg.T], axis=0)


def _cumsum_table(tm, period):
    r = jnp.arange(tm)[:, None]
    s = jnp.arange(tm)[None, :]
    return ((s <= r) & (r // period == s // period)).astype(BF16)


def _row(vec, reps=1):
    return jnp.tile(vec.astype(F32), reps)[None, :]


def _hi_lo(w):
    w = w.astype(F32)
    bits = lax.bitcast_convert_type(w, jnp.uint32)
    bits = (bits + jnp.uint32(0x7FFF) + ((bits >> 16) & jnp.uint32(1))) & jnp.uint32(0xFFFF0000)
    top = lax.bitcast_convert_type(bits, F32)
    return jnp.stack([top.astype(BF16), (w - top).astype(BF16)])


def _pad_lanes(x):
    return jnp.pad(x, [(0, 0)] * (x.ndim - 1) + [(0, LANES - x.shape[-1])])


def kernel(x_prompt, x_sample, p_prompt, p_sample, cache_fox_k, cache_fox_v, cache_fox_logf, cache_diff_k, cache_diff_v, page_table, attn_norm, fox_w_in, fox_b_f, fox_q_norm, fox_k_norm, fox_w_out, diff_w_in, diff_q_norm, diff_k_norm, diff_lambda, diff_subln, diff_w_out, ffn_norm, moe_w_group, moe_b_group, moe_w_expert, moe_b_expert, moe_w_gate, moe_w_up, moe_w_down, ple_norm, ple_w_gate, ple_w_proj):
    b, s, d = x_prompt.shape
    nb, t_new, _ = x_sample.shape
    depth = attn_norm.shape[0]
    n_pool, page = cache_fox_k.shape[1:3]
    past_len = page_table.shape[1] * page
    fox_heads = fox_b_f.shape[-1]
    diff_heads = d // (2 * GROUP)
    n_groups = moe_w_group.shape[-1]
    n_experts = moe_w_expert.shape[-1]
    assert fox_q_norm.shape[-1] == GROUP and diff_q_norm.shape[-1] == GROUP and fox_heads * GROUP == d
    assert 2 * fox_heads <= LANES and n_experts + n_groups <= LANES

    seg64, exp64 = _segment_tables(d, GROUP)
    seg128, exp128 = _segment_tables(d, 2 * GROUP)
    tri_p = _cumsum_table(min(PROJ_ROWS, b * s), s)
    tri_s = _cumsum_table(min(PROJ_ROWS, nb * t_new), t_new)
    rot = GROUP // ROPE_FRACTION
    inv_freq = jnp.power(jnp.float32(ROPE_THETA), -jnp.arange(rot // 2, dtype=jnp.float32) * (2.0 / rot))
    lane_d = jnp.arange(LANES) % GROUP
    invf = jnp.where(lane_d < rot, inv_freq[lane_d % (rot // 2)], 0.0)[None, :]

    fk_p, fv_p, ff_p, dk_p, dv_p = [], [], [], [], []
    fk_s, fv_s, ff_s, dk_s, dv_s = [], [], [], [], []
    hp = x_prompt.reshape(b * s, d)
    hs = x_sample.reshape(nb * t_new, d)
    for i in range(depth):
        j = i // N_MIXERS
        an = _row(attn_norm[i])
        if i % N_MIXERS == 0:
            w_qkv = _hi_lo(fox_w_in[j][:, :3 * d])
            wf = _pad_lanes(jnp.concatenate(list(_hi_lo(fox_w_in[j][:, 3 * d:])), axis=1))
            bf = _pad_lanes(_row(fox_b_f[j]))
            qg, kg = _row(fox_q_norm[j], fox_heads), _row(fox_k_norm[j], fox_heads)
            w_out = _hi_lo(fox_w_out[j])
            qp, kp, vp, lfp, cp, kbp, vbp, ctp = _fox_project(
                hp, an, w_qkv[:1], wf, bf, qg, kg, seg64, exp64, tri_p, period=s, prompt=True)
            qs, ks, vs, lfs, cs = _fox_project(
                hs, an, w_qkv, wf, bf, qg, kg, seg64, exp64, tri_s, period=t_new, prompt=False)
            op = _prompt_attention(qp.reshape(b, s, d), kbp.reshape(b, s, d), vbp.reshape(b, s, d),
                                   (cp.reshape(b, s, fox_heads), ctp), fox=True)
            os_ = _decode_attention(
                page_table, qs.reshape(nb, t_new, d), ks.reshape(nb, t_new, d), vs.reshape(nb, t_new, d),
                cs.reshape(nb, t_new * fox_heads, 1),
                jnp.transpose(cache_fox_k[j], (0, 2, 3, 1)).reshape(n_pool, d, page),
                jnp.transpose(cache_fox_v[j], (0, 2, 3, 1)).reshape(n_pool, d, page),
                jnp.transpose(cache_fox_logf[j], (0, 2, 1)), fox=True)
            hp = _out_project(hp, op.reshape(b * s, d), w_out[:1], diff=False)
            hs = _out_project(hs, os_.reshape(nb * t_new, d), w_out, diff=False)
            fk_p.append(jnp.transpose(kp.reshape(b, fox_heads, GROUP, s), (0, 3, 1, 2)))
            fv_p.append(jnp.transpose(vp.reshape(b, fox_heads, GROUP, s), (0, 3, 1, 2)))
            ff_p.append(jnp.transpose(lfp, (0, 2, 1)))
            fk_s.append(ks.reshape(nb, t_new, fox_heads, GROUP))
            fv_s.append(vs.reshape(nb, t_new, fox_heads, GROUP))
            ff_s.append(lfs.reshape(nb, t_new, fox_heads))
        else:
            lam_init = 0.8 - 0.6 * math.exp(-0.3 * i)
            w_qkv = diff_w_in[j].astype(BF16)
            qg, kg = _row(diff_q_norm[j], 2 * diff_heads), _row(diff_k_norm[j], 2 * diff_heads)
            w_out = diff_w_out[j].astype(BF16)[None]
            subln = _row(diff_subln[j], diff_heads)
            lam = diff_lambda[j].astype(F32)
            qp, kp, vp, kbp, vbp = _diff_project(hp, an, w_qkv, qg, kg, seg64, exp64, invf,
                                                 period=s, offset=0, prompt=True)
            qs, ks, vs = _diff_project(hs, an, w_qkv, qg, kg, seg64, exp64, invf,
                                       period=t_new, offset=past_len, prompt=False)
            op = _prompt_attention(qp.reshape(b, s, d), kbp.reshape(b, s, d), vbp.reshape(b, s, d),
                                   (lam,), fox=False, lam_init=lam_init)
            os_ = _decode_attention(
                page_table, qs.reshape(nb, t_new, d), ks.reshape(nb, t_new, d), vs.reshape(nb, t_new, d),
                lam, jnp.transpose(cache_diff_k[j], (0, 2, 3, 4, 1)).reshape(n_pool, d, page),
                cache_diff_v[j].reshape(n_pool, page * diff_heads, 2 * GROUP), fox=False, lam_init=lam_init)
            hp = _out_project(hp, op.reshape(b * s, d), w_out, subln, seg128, exp128,
                              diff=True, out_scale=1.0 - lam_init)
            hs = _out_project(hs, os_.reshape(nb * t_new, d), w_out, subln, seg128, exp128,
                              diff=True, out_scale=1.0 - lam_init)
            dk_p.append(jnp.transpose(kp.reshape(b, diff_heads, 2, GROUP, s), (0, 4, 1, 2, 3)))
            dv_p.append(vp.reshape(b, s, diff_heads, 2 * GROUP))
            dk_s.append(ks.reshape(nb, t_new, diff_heads, 2, GROUP))
            dv_s.append(vs.reshape(nb, t_new, diff_heads, 2 * GROUP))

        wr = _hi_lo(_pad_lanes(jnp.concatenate([moe_w_expert[i], moe_w_group[i]], axis=1)))
        br = _pad_lanes(_row(jnp.concatenate([moe_b_expert[i], moe_b_group[i]])))
        moe_args = (_row(ffn_norm[i]), wr, br, moe_w_gate[i].astype(BF16), moe_w_up[i].astype(BF16),
                    moe_w_down[i].astype(BF16))
        hp = _moe(hp, *moe_args, n_groups=n_groups)
        hs = _moe(hs, *moe_args, n_groups=n_groups)

        ple_args = (_row(ple_norm[i]), ple_w_gate[i].astype(BF16), ple_w_proj[i].astype(BF16))
        hp = _ple(hp, p_prompt[i].reshape(b * s, -1), *ple_args)
        hs = _ple(hs, p_sample[i].reshape(nb * t_new, -1), *ple_args)

    return (hp.reshape(b, s, d), hs.reshape(nb, t_new, d),
            jnp.stack(fk_p), jnp.stack(fv_p), jnp.stack(ff_p), jnp.stack(dk_p), jnp.stack(dv_p),
            jnp.stack(fk_s), jnp.stack(fv_s), jnp.stack(ff_s), jnp.stack(dk_s), jnp.stack(dv_s))
```
